```python
import math
import jax, jax.numpy as jnp
from jax import lax
import numpy as np

D_MODEL = 4096
BATCH = 2
SEQ = 4096
DEPTH = 2

CTX_LEN = 256
GRID_W = 64
D_HEAD = 128
H_DIFF = 8
DIFF_QK = 2 * D_HEAD
DIFF_V = 2 * D_HEAD
FNET_GROUPS = 8
FNET_GROUP_W = 256
H_GQA = 16
H_KV = 4
BRANCH_W = 2048
N_BRANCH = 3
D_FF = 11008
CONV_W = 3
Q_BLOCK = 128
ROPE_THETA = 10000.0
NORM_EPS = 1e-6
DEEPNORM_ALPHA = (2 * DEPTH) ** 0.25
DEEPNORM_BETA = (8 * DEPTH) ** -0.25

A_Q = H_DIFF * DIFF_QK
A_K = H_DIFF * DIFF_QK
A_V = H_DIFF * DIFF_V
B_W = FNET_GROUPS * FNET_GROUP_W
C_Q = H_GQA * D_HEAD
C_K = H_KV * D_HEAD
C_V = H_KV * D_HEAD
D_IN = A_Q + A_K + A_V + B_W + C_Q + C_K + C_V
SPLIT_POINTS = (A_Q, A_Q + A_K, A_Q + A_K + A_V, A_Q + A_K + A_V + B_W,
                A_Q + A_K + A_V + B_W + C_Q, A_Q + A_K + A_V + B_W + C_Q + C_K)

kernel_name = "hybrid_diffattn_fnet_gqa_convffn_prefix_ctx"


def layer_norm(x, g, b):
    xf = x.astype(jnp.float32)
    mu = jnp.mean(xf, axis=-1, keepdims=True)
    var = jnp.mean(jnp.square(xf - mu), axis=-1, keepdims=True)
    return ((xf - mu) * lax.rsqrt(var + NORM_EPS)).astype(x.dtype) * g + b


def rms_norm(x, g):
    xf = x.astype(jnp.float32)
    return (xf * lax.rsqrt(jnp.mean(xf * xf, axis=-1, keepdims=True) + NORM_EPS)).astype(x.dtype) * g


def axial_rope_tables(n, dtype):
    n_rows = n // GRID_W
    rows = jnp.broadcast_to(jnp.arange(n_rows)[:, None], (n_rows, GRID_W)).reshape(-1).astype(jnp.float32)
    cols = jnp.broadcast_to(jnp.arange(GRID_W)[None, :], (n_rows, GRID_W)).reshape(-1).astype(jnp.float32)
    quarter = D_HEAD // 4
    inv_freq = ROPE_THETA ** (-jnp.arange(quarter, dtype=jnp.float32) / quarter)
    ang = jnp.concatenate([rows[:, None] * inv_freq, cols[:, None] * inv_freq], axis=-1)
    ang = jnp.concatenate([ang, ang], axis=-1)[:, None, :]
    return jnp.cos(ang).astype(dtype), jnp.sin(ang).astype(dtype)


def apply_rope(x, cos, sin):
    x1, x2 = jnp.split(x, 2, axis=-1)
    return x * cos + jnp.concatenate([-x2, x1], axis=-1) * sin


def _to_blocks(q):
    b, n = q.shape[:2]
    return jnp.moveaxis(q.reshape((b, n // Q_BLOCK, Q_BLOCK) + q.shape[2:]), 1, 0)


def _from_blocks(o):
    o = jnp.moveaxis(o, 0, 1)
    return o.reshape((o.shape[0], o.shape[1] * o.shape[2]) + o.shape[3:])


def gqa_attention(q, k, v):
    b, n, h, d = q.shape
    hk = k.shape[2]
    qg = q.reshape(b, n, hk, h // hk, d)
    scale = d ** -0.5

    def block(qi):
        s = jnp.einsum('bqkgd,blkd->bkgql', qi, k).astype(jnp.float32) * scale
        p = jax.nn.softmax(s, axis=-1).astype(v.dtype)
        return jnp.einsum('bkgql,blke->bqkge', p, v)

    o = _from_blocks(lax.map(block, _to_blocks(qg)))
    return o.reshape(b, n, h * v.shape[-1])


def diff_attention(q1, q2, k1, k2, v, lam):
    scale = q1.shape[-1] ** -0.5

    def block(qs):
        qa, qb = qs
        p1 = jax.nn.softmax(jnp.einsum('bqhd,blhd->bhql', qa, k1).astype(jnp.float32) * scale, axis=-1)
        p2 = jax.nn.softmax(jnp.einsum('bqhd,blhd->bhql', qb, k2).astype(jnp.float32) * scale, axis=-1)
        p = (p1 - lam * p2).astype(v.dtype)
        return jnp.einsum('bhql,blhe->bqhe', p, v)

    return _from_blocks(lax.map(block, (_to_blocks(q1), _to_blocks(q2))))


def diff_head_norm(o, g, lam_init):
    b, n = o.shape[:2]
    return (rms_norm(o, g) * (1.0 - lam_init)).reshape(b, n, A_V)


def fourier_mix(u):
    b, n, _ = u.shape
    ug = u.reshape(b, n, FNET_GROUPS, FNET_GROUP_W).astype(jnp.float32)
    f = jnp.fft.fft2(ug, axes=(1, 3), norm='ortho').real
    return f.reshape(b, n, B_W).astype(u.dtype)


def split_proj(p):
    b, n, _ = p.shape
    aq, ak, av, bu, cq, ck, cv = jnp.split(p, SPLIT_POINTS, axis=-1)
    aq = aq.reshape(b, n, H_DIFF, 2, D_HEAD)
    ak = ak.reshape(b, n, H_DIFF, 2, D_HEAD)
    av = av.reshape(b, n, H_DIFF, DIFF_V)
    cq = cq.reshape(b, n, H_GQA, D_HEAD)
    ck = ck.reshape(b, n, H_KV, D_HEAD)
    cv = cv.reshape(b, n, H_KV, D_HEAD)
    return aq[..., 0, :], aq[..., 1, :], ak[..., 0, :], ak[..., 1, :], av, bu, cq, ck, cv


def gated_merge(h, oa, ob, oc, w_branch, w_gate, b_gate, w_o):
    br = jnp.stack([oa, ob, oc], axis=2)
    pb = jnp.einsum('bsnk,nkd->bsnd', br, w_branch)
    g = jax.nn.sigmoid(h @ w_gate + b_gate).reshape(pb.shape)
    return jnp.sum(g * pb, axis=2) @ w_o


def token_mixer(hl, hc, need_ctx_out, cos, sin, w_in, lam, lam_init, subln_g, qk_g,
                w_branch, w_gate, b_gate, w_o):
    lq1, lq2, lk1, lk2, lva, lu, lqg, lkg, lvg = split_proj(hl @ w_in)
    cq1, cq2, ck1, ck2, cva, cu, cqg, ckg, cvg = split_proj(hc @ w_in)
    lq1, lq2, lk1, lk2 = (apply_rope(t, cos, sin) for t in (lq1, lq2, lk1, lk2))
    lqg = apply_rope(rms_norm(lqg, qk_g[0]), cos, sin)
    lkg = apply_rope(rms_norm(lkg, qk_g[1]), cos, sin)
    cqg = rms_norm(cqg, qk_g[0])
    ckg = rms_norm(ckg, qk_g[1])

    def cat(a, b):
        return jnp.concatenate([a, b], axis=1)

    oa_l = diff_head_norm(diff_attention(lq1, lq2, cat(ck1, lk1), cat(ck2, lk2), cat(cva, lva), lam),
                          subln_g, lam_init)
    ob_l = fourier_mix(lu)
    oc_l = gqa_attention(lqg, cat(ckg, lkg), cat(cvg, lvg))
    out_l = gated_merge(hl, oa_l, ob_l, oc_l, w_branch, w_gate, b_gate, w_o)
    if not need_ctx_out:
        return out_l, None
    oa_c = diff_head_norm(diff_attention(cq1, cq2, ck1, ck2, cva, lam), subln_g, lam_init)
    ob_c = fourier_mix(cu)
    oc_c = gqa_attention(cqg, ckg, cvg)
    out_c = gated_merge(hc, oa_c, ob_c, oc_c, w_branch, w_gate, b_gate, w_o)
    return out_l, out_c


def dwconv3(a, w, b):
    ap = jnp.pad(a, ((0, 0), (1, 1), (0, 0)))
    return ap[:, :-2] * w[0] + ap[:, 1:-1] * w[1] + ap[:, 2:] * w[2] + b


def conv_ffn(h, w_up, conv_w, conv_b, w_down):
    a, g = jnp.split(h @ w_up, 2, axis=-1)
    return (jax.nn.gelu(dwconv3(a, conv_w, conv_b), approximate=False) * g) @ w_down


def setup_inputs(seed: int = 0) -> dict:
    key = jax.random.key(seed)
    ks = jax.random.split(key, 22)
    f32 = jnp.float32

    def nrm(k, shape, s):
        return jax.random.normal(k, shape, f32) * s

    d = D_MODEL
    return {
        'x': nrm(ks[0], (BATCH, SEQ, d), 1.0),
        'c': nrm(ks[1], (BATCH, d), 1.0),
        'ctx': nrm(ks[2], (BATCH, CTX_LEN, d), 1.0),
        'c_ctx': nrm(ks[3], (d,), 1.0),
        'w_ada': nrm(ks[4], (DEPTH, d, 6 * d), d ** -0.5),
        'b_ada': nrm(ks[5], (DEPTH, 6 * d), 0.02),
        'w_in': nrm(ks[6], (DEPTH, d, D_IN), d ** -0.5),
        'diff_lambda': nrm(ks[7], (DEPTH, 4, D_HEAD), 0.1),
        'diff_subln_g': 1.0 + nrm(ks[8], (DEPTH, DIFF_V), 0.02),
        'qk_norm_g': 1.0 + nrm(ks[9], (DEPTH, 2, D_HEAD), 0.02),
        'w_branch': nrm(ks[10], (DEPTH, N_BRANCH, BRANCH_W, d), BRANCH_W ** -0.5),
        'w_gate': nrm(ks[11], (DEPTH, d, N_BRANCH * d), d ** -0.5),
        'b_gate': nrm(ks[12], (DEPTH, N_BRANCH * d), 0.02),
        'w_o': nrm(ks[13], (DEPTH, d, d), d ** -0.5 * DEEPNORM_BETA),
        'ln1_g': 1.0 + nrm(ks[14], (DEPTH, d), 0.02),
        'ln1_b': nrm(ks[15], (DEPTH, d), 0.02),
        'w_up': nrm(ks[16], (DEPTH, d, 2 * D_FF), d ** -0.5),
        'conv_w': nrm(ks[17], (DEPTH, CONV_W, D_FF), CONV_W ** -0.5),
        'conv_b': nrm(ks[18], (DEPTH, D_FF), 0.02),
        'w_down': nrm(ks[19], (DEPTH, D_FF, d), D_FF ** -0.5 * DEEPNORM_BETA),
        'ln2_g': 1.0 + nrm(ks[20], (DEPTH, d), 0.02),
        'ln2_b': nrm(ks[21], (DEPTH, d), 0.02),
    }


def reference(x, c, ctx, c_ctx, w_ada, b_ada, w_in, diff_lambda, diff_subln_g, qk_norm_g,
              w_branch, w_gate, b_gate, w_o, ln1_g, ln1_b, w_up, conv_w, conv_b, w_down,
              ln2_g, ln2_b):
    cos, sin = axial_rope_tables(x.shape[1], x.dtype)
    xl, xc = x, ctx
    for l in range(DEPTH):
        last = l == DEPTH - 1
        lam_init = 0.8 - 0.6 * math.exp(-0.3 * l)
        ml = jnp.split((jax.nn.silu(c) @ w_ada[l] + b_ada[l])[:, None, :], 6, axis=-1)
        mc = jnp.split(jax.nn.silu(c_ctx) @ w_ada[l] + b_ada[l], 6, axis=-1)
        lp = diff_lambda[l].astype(jnp.float32)
        lam = jnp.exp(jnp.sum(lp[0] * lp[1])) - jnp.exp(jnp.sum(lp[2] * lp[3])) + lam_init

        hl = xl * (1 + ml[1]) + ml[0]
        hc = xc * (1 + mc[1]) + mc[0]
        ol, oc = token_mixer(hl, hc, not last, cos, sin, w_in[l], lam, lam_init, diff_subln_g[l],
                             qk_norm_g[l], w_branch[l], w_gate[l], b_gate[l], w_o[l])
        xl = layer_norm(DEEPNORM_ALPHA * xl + ml[2] * ol, ln1_g[l], ln1_b[l])

        hl = xl * (1 + ml[4]) + ml[3]
        xl = layer_norm(DEEPNORM_ALPHA * xl + ml[5] * conv_ffn(hl, w_up[l], conv_w[l], conv_b[l], w_down[l]),
                        ln2_g[l], ln2_b[l])

        if not last:
            xc = layer_norm(DEEPNORM_ALPHA * xc + mc[2] * oc, ln1_g[l], ln1_b[l])
            hc = xc * (1 + mc[4]) + mc[3]
            xc = layer_norm(DEEPNORM_ALPHA * xc + mc[5] * conv_ffn(hc, w_up[l], conv_w[l], conv_b[l], w_down[l]),
                            ln2_g[l], ln2_b[l])
    return xl
```

```python
import functools
import math

import jax
import jax.numpy as jnp
from jax import lax
from jax.experimental import pallas as pl
from jax.experimental.pallas import tpu as pltpu

F32 = jnp.float32
BF16 = jnp.bfloat16

GRID_W = 64
D_HEAD = 128
H_DIFF = 8
DIFF_W = 2 * D_HEAD
FNET_GROUPS = 8
FNET_GROUP_W = 256
H_GQA = 16
H_KV = 4
GQA_GROUP = H_GQA // H_KV
BRANCH_W = 2048
N_BRANCH = 3
ROPE_THETA = 10000.0
NORM_EPS = 1e-6

OFF_AQ = 0
OFF_AK = OFF_AQ + H_DIFF * DIFF_W
OFF_AV = OFF_AK + H_DIFF * DIFF_W
OFF_B = OFF_AV + H_DIFF * DIFF_W
OFF_CQ = OFF_B + FNET_GROUPS * FNET_GROUP_W
OFF_CK = OFF_CQ + H_GQA * D_HEAD
OFF_CV = OFF_CK + H_KV * D_HEAD
D_IN = OFF_CV + H_KV * D_HEAD
QK_SCALE = D_HEAD ** -0.5

V7X_VMEM_LIMIT_BYTES = 56 * 1024 * 1024
LANES = 128
IN_TILE = 512


def _params(sem):
    return pltpu.CompilerParams(dimension_semantics=sem, vmem_limit_bytes=V7X_VMEM_LIMIT_BYTES)


def _pick(dim, pref, align):
    t = (min(pref, dim) // align) * align
    while t >= align:
        if dim % t == 0:
            return t
        t -= align
    return dim


def _dot(a, b):
    return jnp.dot(a, b, preferred_element_type=F32)


def _dot_nt(a, b):
    return lax.dot_general(a, b, (((1,), (1,)), ((), ())), preferred_element_type=F32)


def _ada_kernel(c_ref, w_ref, b_ref, o_ref):
    c = c_ref[...]
    s = c * jax.nn.sigmoid(c)
    o_ref[0] = jnp.dot(s, w_ref[0], precision=lax.Precision.HIGHEST,
                       preferred_element_type=F32) + b_ref[0]


def _ada(cvec, w_ada, b_ada):
    depth, d, n = w_ada.shape
    bn = _pick(n, 512, LANES)
    return pl.pallas_call(
        _ada_kernel,
        grid=(depth, n // bn),
        in_specs=[pl.BlockSpec((8, d), lambda l, j: (0, 0)),
                  pl.BlockSpec((1, d, bn), lambda l, j: (l, 0, j)),
                  pl.BlockSpec((1, 1, bn), lambda l, j: (l, 0, j))],
        out_specs=pl.BlockSpec((1, 8, bn), lambda l, j: (l, 0, j)),
        out_shape=jax.ShapeDtypeStruct((depth, 8, n), F32),
        compiler_params=_params(("arbitrary", "arbitrary")),
        name="ada",
    )(cvec, w_ada, b_ada.reshape(depth, 1, n))


def _mod_kernel(x_ref, sc_ref, sh_ref, h_ref):
    h_ref[...] = (x_ref[...] * (1.0 + sc_ref[0]) + sh_ref[0]).astype(BF16)


def _modulate(x, sc, sh, rows_per_group):
    m, d = x.shape
    bs = _pick(rows_per_group, 256, 8)
    grp = lambda i: (i * bs // rows_per_group, 0, 0)
    return pl.pallas_call(
        _mod_kernel,
        grid=(m // bs,),
        in_specs=[pl.BlockSpec((bs, d), lambda i: (i, 0)),
                  pl.BlockSpec((1, 1, d), grp),
                  pl.BlockSpec((1, 1, d), grp)],
        out_specs=pl.BlockSpec((bs, d), lambda i: (i, 0)),
        out_shape=jax.ShapeDtypeStruct((m, d), BF16),
        compiler_params=_params(("arbitrary",)),
        name="modulate",
    )(x, sc, sh)


def _layer_norm(y, g, b):
    mu = jnp.mean(y, axis=-1, keepdims=True)
    dlt = y - mu
    var = jnp.mean(dlt * dlt, axis=-1, keepdims=True)
    return dlt * lax.rsqrt(var + NORM_EPS) * g + b


def _ln_mod_kernel(y_ref, g_ref, b_ref, sc_ref, sh_ref, x_ref, h_ref):
    xn = _layer_norm(y_ref[...], g_ref[...], b_ref[...])
    x_ref[...] = xn
    h_ref[...] = (xn * (1.0 + sc_ref[0]) + sh_ref[0]).astype(BF16)


def _ln_kernel(y_ref, g_ref, b_ref, x_ref):
    x_ref[...] = _layer_norm(y_ref[...], g_ref[...], b_ref[...])


def _ln(y, g, b, mod=None, rows_per_group=None):
    m, d = y.shape
    bs = _pick(rows_per_group or m, 256, 8)
    row = pl.BlockSpec((bs, d), lambda i: (i, 0))
    vec = pl.BlockSpec((1, d), lambda i: (0, 0))
    g2, b2 = g.reshape(1, d), b.reshape(1, d)
    if mod is None:
        return pl.pallas_call(
            _ln_kernel, grid=(m // bs,), in_specs=[row, vec, vec], out_specs=row,
            out_shape=jax.ShapeDtypeStruct((m, d), F32),
            compiler_params=_params(("arbitrary",)), name="layernorm",
        )(y, g2, b2)
    grp = pl.BlockSpec((1, 1, d), lambda i: (i * bs // rows_per_group, 0, 0))
    return pl.pallas_call(
        _ln_mod_kernel, grid=(m // bs,), in_specs=[row, vec, vec, grp, grp],
        out_specs=[row, row],
        out_shape=[jax.ShapeDtypeStruct((m, d), F32), jax.ShapeDtypeStruct((m, d), BF16)],
        compiler_params=_params(("arbitrary",)), name="layernorm_mod",
    )(y, g2, b2, mod[0], mod[1])


def _mm_kernel(*refs, nk, n_extra, epilogue):
    x_ref, w_ref = refs[0], refs[1]
    extra = refs[2:2 + n_extra]
    o_ref = refs[2 + n_extra]
    part = _dot(x_ref[...], w_ref[...])
    if nk == 1:
        o_ref[...] = epilogue(part, *extra).astype(o_ref.dtype)
        return
    acc_ref = refs[3 + n_extra]
    k = pl.program_id(2)

    @pl.when(k == 0)
    def _():
        acc_ref[...] = part

    @pl.when((k > 0) & (k < nk - 1))
    def _():
        acc_ref[...] += part

    @pl.when(k == nk - 1)
    def _():
        o_ref[...] = epilogue(acc_ref[...] + part, *extra).astype(o_ref.dtype)


def _matmul(x, w, *, bm, bn, bk, out_dtype, epilogue, extra=(), extra_specs=(), name):
    m, kdim = x.shape
    n = w.shape[1]
    nk = kdim // bk
    kernel = functools.partial(_mm_kernel, nk=nk, n_extra=len(extra), epilogue=epilogue)
    return pl.pallas_call(
        kernel,
        grid=(m // bm, n // bn, nk),
        in_specs=[pl.BlockSpec((bm, bk), lambda i, j, k: (i, k)),
                  pl.BlockSpec((bk, bn), lambda i, j, k: (k, j)),
                  *extra_specs],
        out_specs=pl.BlockSpec((bm, bn), lambda i, j, k: (i, j)),
        out_shape=jax.ShapeDtypeStruct((m, n), out_dtype),
        scratch_shapes=[pltpu.VMEM((bm, bn), F32)] if nk > 1 else [],
        compiler_params=_params(("arbitrary", "arbitrary", "arbitrary")),
        name=name,
    )(x, w, *extra)


def _identity(acc):
    return acc


def _residual_epilogue(acc, x_ref, gate_ref, *, alpha):
    return alpha * x_ref[...] + gate_ref[0] * acc


def _matmul_residual(a, w, xres, gate, rows_per_group, alpha, *, bk, name):
    m = a.shape[0]
    n = w.shape[1]
    bm = _pick(rows_per_group, 1024, 8)
    bn = _pick(n, 1024, LANES)
    return _matmul(
        a, w, bm=bm, bn=bn, bk=bk, out_dtype=F32,
        epilogue=functools.partial(_residual_epilogue, alpha=alpha),
        extra=(xres, gate),
        extra_specs=(pl.BlockSpec((bm, bn), lambda i, j, k: (i, j)),
                     pl.BlockSpec((1, 1, bn), lambda i, j, k: (i * bm // rows_per_group, 0, j))),
        name=name)


def _rope(x, cos, sin_signed):
    return x * cos + pltpu.roll(x, D_HEAD // 2, 1) * sin_signed


def _inproj_kernel(h_ref, w_ref, cos_ref, sin_ref, g_ref, o_ref):
    j = pl.program_id(1)
    acc = _dot(h_ref[...], w_ref[...])
    n_chunks = IN_TILE // D_HEAD
    t_ak, t_av, t_cq, t_ck, t_cv = (OFF_AK // IN_TILE, OFF_AV // IN_TILE, OFF_CQ // IN_TILE,
                                    OFF_CK // IN_TILE, OFF_CV // IN_TILE)

    @pl.when(j < t_av)
    def _():
        s = jnp.where(j < t_ak, QK_SCALE, 1.0).astype(F32)
        cos = cos_ref[...] * s
        sin = sin_ref[...] * s
        for c in range(n_chunks):
            sl = slice(c * D_HEAD, (c + 1) * D_HEAD)
            o_ref[:, sl] = _rope(acc[:, sl], cos, sin).astype(BF16)

    @pl.when(((j >= t_av) & (j < t_cq)) | (j >= t_cv))
    def _():
        o_ref[...] = acc.astype(BF16)

    @pl.when((j >= t_cq) & (j < t_cv))
    def _():
        is_q = j < t_ck
        g = jnp.where(is_q, g_ref[0:1, :], g_ref[1:2, :])
        s = jnp.where(is_q, QK_SCALE, 1.0).astype(F32)
        cos = cos_ref[...] * s
        sin = sin_ref[...] * s
        for c in range(n_chunks):
            sl = slice(c * D_HEAD, (c + 1) * D_HEAD)
            x = acc[:, sl]
            ms = jnp.mean(x * x, axis=-1, keepdims=True)
            xn = x * lax.rsqrt(ms + NORM_EPS) * g
            o_ref[:, sl] = _rope(xn, cos, sin).astype(BF16)


def _inproj(h, w, cos, sin_signed, qk_g, rows_per_seq):
    m, d = h.shape
    bm = _pick(rows_per_seq, 1024, 8)
    nseq_blk = rows_per_seq // bm
    tab = pl.BlockSpec((bm, D_HEAD), lambda i, j: (i % nseq_blk, 0))
    return pl.pallas_call(
        _inproj_kernel,
        grid=(m // bm, D_IN // IN_TILE),
        in_specs=[pl.BlockSpec((bm, d), lambda i, j: (i, 0)),
                  pl.BlockSpec((d, IN_TILE), lambda i, j: (0, j)),
                  tab, tab,
                  pl.BlockSpec((2, D_HEAD), lambda i, j: (0, 0))],
        out_specs=pl.BlockSpec((bm, IN_TILE), lambda i, j: (i, j)),
        out_shape=jax.ShapeDtypeStruct((m, D_IN), BF16),
        compiler_params=_params(("arbitrary", "arbitrary")),
        name="inproj",
    )(h, w, cos, sin_signed, qk_g)


def _softmax_parts(q, k_refs, lo):
    scores = [_dot_nt(q, k_ref[:, lo:lo + D_HEAD]) for k_ref in k_refs]
    mx = functools.reduce(jnp.maximum, [jnp.max(s, axis=-1, keepdims=True) for s in scores])
    es = [jnp.exp(s - mx) for s in scores]
    den = functools.reduce(jnp.add, [jnp.sum(e, axis=-1, keepdims=True) for e in es])
    return es, den


def _diff_attn_kernel(*refs, n_src, lam_init):
    q_ref = refs[0]
    k_refs = refs[1:1 + n_src]
    v_refs = refs[1 + n_src:1 + 2 * n_src]
    lam_ref, g_ref, o_ref = refs[1 + 2 * n_src:]
    lp = lam_ref[...]
    lam = (jnp.exp(jnp.sum(lp[0:1] * lp[1:2], axis=-1, keepdims=True))
           - jnp.exp(jnp.sum(lp[2:3] * lp[3:4], axis=-1, keepdims=True)) + lam_init)
    q = q_ref[...]
    e1, den1 = _softmax_parts(q[:, :D_HEAD], k_refs, 0)
    e2, den2 = _softmax_parts(q[:, D_HEAD:], k_refs, D_HEAD)
    a1 = 1.0 / den1
    a2 = lam / den2
    o = None
    for x1, x2, v_ref in zip(e1, e2, v_refs):
        p = (x1 * a1 - x2 * a2).astype(BF16)
        pv = _dot(p, v_ref[...])
        o = pv if o is None else o + pv
    ms = jnp.mean(o * o, axis=-1, keepdims=True)
    o_ref[...] = ((o * lax.rsqrt(ms + NORM_EPS)) * g_ref[...] * (1.0 - lam_init)).astype(BF16)


def _diff_attention(pq, kv_sources, lam_p, subln_g, lam_init, batch, q_len, bq):
    nq = q_len // bq
    kblk, vblk = OFF_AK // DIFF_W, OFF_AV // DIFF_W
    k_specs = [pl.BlockSpec((n, DIFF_W), lambda b, h, i: (b, kblk + h)) for _, n in kv_sources]
    v_specs = [pl.BlockSpec((n, DIFF_W), lambda b, h, i: (b, vblk + h)) for _, n in kv_sources]
    srcs = [p for p, _ in kv_sources]
    kernel = functools.partial(_diff_attn_kernel, n_src=len(srcs), lam_init=lam_init)
    return pl.pallas_call(
        kernel,
        grid=(batch, H_DIFF, nq),
        in_specs=[pl.BlockSpec((bq, DIFF_W), lambda b, h, i: (b * nq + i, h)),
                  *k_specs, *v_specs,
                  pl.BlockSpec((4, D_HEAD), lambda b, h, i: (0, 0)),
                  pl.BlockSpec((1, DIFF_W), lambda b, h, i: (0, 0))],
        out_specs=pl.BlockSpec((bq, DIFF_W), lambda b, h, i: (b * nq + i, h)),
        out_shape=jax.ShapeDtypeStruct((batch * q_len, H_DIFF * DIFF_W), BF16),
        compiler_params=_params(("arbitrary", "arbitrary", "arbitrary")),
        name="diff_attention",
    )(pq, *srcs, *srcs, lam_p, subln_g.reshape(1, DIFF_W))


def _gqa_kernel(*refs, n_src):
    q_ref = refs[0]
    k_refs = refs[1:1 + n_src]
    v_refs = refs[1 + n_src:1 + 2 * n_src]
    o_ref = refs[1 + 2 * n_src]
    es, den = _softmax_parts(q_ref[...], k_refs, 0)
    o = None
    for e, v_ref in zip(es, v_refs):
        pv = _dot(e.astype(BF16), v_ref[...])
        o = pv if o is None else o + pv
    o_ref[...] = (o / den).astype(BF16)


def _gqa_attention(pq, kv_sources, batch, q_len, bq):
    nq = q_len // bq
    qblk, kblk, vblk = OFF_CQ // D_HEAD, OFF_CK // D_HEAD, OFF_CV // D_HEAD
    k_specs = [pl.BlockSpec((n, D_HEAD), lambda b, kh, i, g: (b, kblk + kh)) for _, n in kv_sources]
    v_specs = [pl.BlockSpec((n, D_HEAD), lambda b, kh, i, g: (b, vblk + kh)) for _, n in kv_sources]
    srcs = [p for p, _ in kv_sources]
    return pl.pallas_call(
        functools.partial(_gqa_kernel, n_src=len(srcs)),
        grid=(batch, H_KV, nq, GQA_GROUP),
        in_specs=[pl.BlockSpec((bq, D_HEAD), lambda b, kh, i, g: (b * nq + i, qblk + kh * GQA_GROUP + g)),
                  *k_specs, *v_specs],
        out_specs=pl.BlockSpec((bq, D_HEAD), lambda b, kh, i, g: (b * nq + i, kh * GQA_GROUP + g)),
        out_shape=jax.ShapeDtypeStruct((batch * q_len, H_GQA * D_HEAD), BF16),
        compiler_params=_params(("arbitrary",) * 4),
        name="gqa_attention",
    )(pq, *srcs, *srcs)


def _dft_tables(n):
    k = jnp.arange(n, dtype=jnp.int32)
    ang = ((k[:, None] * k[None, :]) % n).astype(F32) * (2.0 * math.pi / n)
    return jnp.cos(ang).astype(BF16), jnp.sin(ang).astype(BF16)


def _fnet_kernel(cn_ref, sn_ref, u_ref, cc_ref, sc_ref, o_ref, *, scale):
    u = u_ref[...]
    t1 = _dot(cn_ref[...], u).astype(BF16)
    t2 = _dot(sn_ref[...], u).astype(BF16)
    for g in range(u.shape[1] // FNET_GROUP_W):
        sl = slice(g * FNET_GROUP_W, (g + 1) * FNET_GROUP_W)
        y = _dot(t1[:, sl], cc_ref[...]) - _dot(t2[:, sl], sc_ref[...])
        o_ref[:, sl] = (y * scale).astype(BF16)


def _fourier_mix(p, batch, n, pos_tabs, chan_tabs):
    bm = _pick(n, 1024, 8)
    bc = 2 * FNET_GROUP_W
    ni = n // bm
    ublk = OFF_B // bc
    pos = pl.BlockSpec((bm, n), lambda b, i, j: (i, 0))
    chan = pl.BlockSpec((FNET_GROUP_W, FNET_GROUP_W), lambda b, i, j: (0, 0))
    scale = 1.0 / math.sqrt(n * FNET_GROUP_W)
    return pl.pallas_call(
        functools.partial(_fnet_kernel, scale=scale),
        grid=(batch, ni, FNET_GROUPS * FNET_GROUP_W // bc),
        in_specs=[pos, pos, pl.BlockSpec((n, bc), lambda b, i, j: (b, ublk + j)), chan, chan],
        out_specs=pl.BlockSpec((bm, bc), lambda b, i, j: (b * ni + i, j)),
        out_shape=jax.ShapeDtypeStruct((batch * n, FNET_GROUPS * FNET_GROUP_W), BF16),
        compiler_params=_params(("arbitrary",) * 3),
        name="fourier_mix",
    )(*pos_tabs, p, *chan_tabs)


def _merge_kernel(h_ref, oa_ref, ob_ref, oc_ref, wg_ref, bg_ref, wb_ref, o_ref, acc_ref):
    n = pl.program_id(2)
    gate = jax.nn.sigmoid(_dot(h_ref[...], wg_ref[...]) + bg_ref[...])

    @pl.when(n == 0)
    def _():
        acc_ref[...] = gate * _dot(oa_ref[...], wb_ref[0])

    @pl.when(n == 1)
    def _():
        acc_ref[...] += gate * _dot(ob_ref[...], wb_ref[0])

    @pl.when(n == 2)
    def _():
        o_ref[...] = (acc_ref[...] + gate * _dot(oc_ref[...], wb_ref[0])).astype(BF16)


def _gated_merge(h, oa, ob, oc, w_gate, b_gate, w_branch):
    m, d = h.shape
    bm = _pick(m, 512, 8)
    bn = _pick(d, 512, LANES)
    nj = d // bn
    br = pl.BlockSpec((bm, BRANCH_W), lambda i, j, n: (i, 0))
    return pl.pallas_call(
        _merge_kernel,
        grid=(m // bm, nj, N_BRANCH),
        in_specs=[pl.BlockSpec((bm, d), lambda i, j, n: (i, 0)), br, br, br,
                  pl.BlockSpec((d, bn), lambda i, j, n: (0, n * nj + j)),
                  pl.BlockSpec((1, bn), lambda i, j, n: (0, n * nj + j)),
                  pl.BlockSpec((1, BRANCH_W, bn), lambda i, j, n: (n, 0, j))],
        out_specs=pl.BlockSpec((bm, bn), lambda i, j, n: (i, j)),
        out_shape=jax.ShapeDtypeStruct((m, d), BF16),
        scratch_shapes=[pltpu.VMEM((bm, bn), F32)],
        compiler_params=_params(("arbitrary",) * 3),
        name="gated_merge",
    )(h, oa, ob, oc, w_gate, b_gate.reshape(1, -1), w_branch)


def _conv_act_kernel(a_ref, g_ref, w_ref, b_ref, o_ref):
    a = a_ref[0]
    s = a.shape[0]
    row = lax.broadcasted_iota(jnp.int32, a.shape, 0)
    prev = jnp.where(row == 0, 0.0, pltpu.roll(a, 1, 0))
    nxt = jnp.where(row == s - 1, 0.0, pltpu.roll(a, s - 1, 0))
    w = w_ref[...]
    z = prev * w[0:1] + a * w[1:2] + nxt * w[2:3] + b_ref[...]
    gelu = 0.5 * z * (1.0 + lax.erf(z * math.sqrt(0.5)))
    o_ref[0] = (gelu * g_ref[0]).astype(BF16)


def _conv_act(u, conv_w, conv_b, batch, s):
    ffp = conv_w.shape[1]
    bc = _pick(ffp, 256, LANES)
    nj = ffp // bc
    u3 = u.reshape(batch, s, 2 * ffp)
    out = pl.pallas_call(
        _conv_act_kernel,
        grid=(batch, nj),
        in_specs=[pl.BlockSpec((1, s, bc), lambda b, j: (b, 0, j)),
                  pl.BlockSpec((1, s, bc), lambda b, j: (b, 0, nj + j)),
                  pl.BlockSpec((3, bc), lambda b, j: (0, j)),
                  pl.BlockSpec((1, bc), lambda b, j: (0, j))],
        out_specs=pl.BlockSpec((1, s, bc), lambda b, j: (b, 0, j)),
        out_shape=jax.ShapeDtypeStruct((batch, s, ffp), BF16),
        compiler_params=_params(("arbitrary", "arbitrary")),
        name="conv_act",
    )(u3, u3, conv_w, conv_b)
    return out.reshape(batch * s, ffp)


def _rope_tables(n):
    n_rows = n // GRID_W
    rows = jnp.broadcast_to(jnp.arange(n_rows)[:, None], (n_rows, GRID_W)).reshape(-1).astype(F32)
    cols = jnp.broadcast_to(jnp.arange(GRID_W)[None, :], (n_rows, GRID_W)).reshape(-1).astype(F32)
    quarter = D_HEAD // 4
    inv_freq = ROPE_THETA ** (-jnp.arange(quarter, dtype=F32) / quarter)
    ang = jnp.concatenate([rows[:, None] * inv_freq, cols[:, None] * inv_freq], axis=-1)
    ang = jnp.concatenate([ang, ang], axis=-1)
    sign = jnp.where(jnp.arange(D_HEAD) < D_HEAD // 2, -1.0, 1.0).astype(F32)
    return jnp.cos(ang), jnp.sin(ang) * sign


def kernel(x, c, ctx, c_ctx, w_ada, b_ada, w_in, diff_lambda, diff_subln_g, qk_norm_g, w_branch, w_gate,
           b_gate, w_o, ln1_g, ln1_b, w_up, conv_w, conv_b, w_down, ln2_g, ln2_b):
    batch, seq, d = x.shape
    n_ctx = ctx.shape[1]
    depth = w_ada.shape[0]
    ff = conv_w.shape[2]
    ffp = -(-ff // 1024) * 1024 if ff > 1024 else -(-ff // LANES) * LANES
    alpha = (2 * depth) ** 0.25

    cvec = jnp.zeros((8, d), F32).at[:batch].set(c).at[batch].set(c_ctx)
    mod = _ada(cvec, w_ada, b_ada).reshape(depth, 8, 6, d)

    cos_l, sin_l = _rope_tables(seq)
    cos_c, sin_c = jnp.ones((n_ctx, D_HEAD), F32), jnp.zeros((n_ctx, D_HEAD), F32)
    pos_l, pos_c = _dft_tables(seq), _dft_tables(n_ctx)
    chan = _dft_tables(FNET_GROUP_W)

    xl = x.reshape(batch * seq, d)
    xc = ctx.reshape(batch * n_ctx, d)
    m_ctx = batch * n_ctx
    bq_l = _pick(seq, 256, 8)
    bq_c = _pick(n_ctx, 256, 8)
    bk_ff = _pick(ffp, 2816, LANES)

    def lat(i, l):
        return mod[l, :batch, i][:, None, :]

    def cx(i, l):
        return mod[l, batch, i][None, None, :]

    hl = _modulate(xl, lat(1, 0), lat(0, 0), seq)
    hc = _modulate(xc, cx(1, 0), cx(0, 0), m_ctx)

    for l in range(depth):
        last = l == depth - 1
        lam_init = 0.8 - 0.6 * math.exp(-0.3 * l)
        w_in_b = w_in[l].astype(BF16)
        w_gate_b = w_gate[l].astype(BF16)
        w_branch_b = w_branch[l].astype(BF16)
        w_o_b = w_o[l].astype(BF16)
        pad = ffp - ff
        w_up_b = jnp.pad(w_up[l].astype(BF16).reshape(d, 2, ff), ((0, 0), (0, 0), (0, pad))).reshape(d, 2 * ffp)
        w_down_b = jnp.pad(w_down[l].astype(BF16), ((0, pad), (0, 0)))
        conv_w_p = jnp.pad(conv_w[l], ((0, 0), (0, pad)))
        conv_b_p = jnp.pad(conv_b[l], ((0, pad),)).reshape(1, ffp)

        pl_ = _inproj(hl, w_in_b, cos_l, sin_l, qk_norm_g[l], seq)
        pc_ = _inproj(hc, w_in_b, cos_c, sin_c, qk_norm_g[l], n_ctx)
        kv = [(pl_, seq), (pc_, n_ctx)]
        oa = _diff_attention(pl_, kv, diff_lambda[l], diff_subln_g[l], lam_init, batch, seq, bq_l)
        ob = _fourier_mix(pl_, batch, seq, pos_l, chan)
        oc = _gqa_attention(pl_, kv, batch, seq, bq_l)
        merged = _gated_merge(hl, oa, ob, oc, w_gate_b, b_gate[l], w_branch_b)
        yl = _matmul_residual(merged, w_o_b, xl, lat(2, l), seq, alpha, bk=d, name="out_proj")
        xl, hl = _ln(yl, ln1_g[l], ln1_b[l], mod=(lat(4, l), lat(3, l)), rows_per_group=seq)

        if not last:
            kvc = [(pc_, n_ctx)]
            oa_c = _diff_attention(pc_, kvc, diff_lambda[l], diff_subln_g[l], lam_init, batch, n_ctx, bq_c)
            ob_c = _fourier_mix(pc_, batch, n_ctx, pos_c, chan)
            oc_c = _gqa_attention(pc_, kvc, batch, n_ctx, bq_c)
            merged_c = _gated_merge(hc, oa_c, ob_c, oc_c, w_gate_b, b_gate[l], w_branch_b)
            yc = _matmul_residual(merged_c, w_o_b, xc, cx(2, l), m_ctx, alpha, bk=d, name="out_proj_ctx")
            xc, hc = _ln(yc, ln1_g[l], ln1_b[l], mod=(cx(4, l), cx(3, l)), rows_per_group=m_ctx)

        def ffn(h, xres, gate, rows_per_group, s):
            mrows = h.shape[0]
            u = _matmul(h, w_up_b, bm=_pick(rows_per_group, 1024, 8), bn=_pick(2 * ffp, 1024, LANES), bk=d,
                        out_dtype=F32, epilogue=_identity, name="ffn_up")
            act = _conv_act(u, conv_w_p, conv_b_p, mrows // s, s)
            return _matmul_residual(act, w_down_b, xres, gate, rows_per_group, alpha, bk=bk_ff, name="ffn_down")

        yl = ffn(hl, xl, lat(5, l), seq, seq)
        if last:
            xl = _ln(yl, ln2_g[l], ln2_b[l])
        else:
            xl, hl = _ln(yl, ln2_g[l], ln2_b[l], mod=(lat(1, l + 1), lat(0, l + 1)), rows_per_group=seq)
            yc = ffn(hc, xc, cx(5, l), m_ctx, n_ctx)
            xc, hc = _ln(yc, ln2_g[l], ln2_b[l], mod=(cx(1, l + 1), cx(0, l + 1)), rows_per_group=m_ctx)

    return xl.reshape(batch, seq, d)
```

```python
import functools
import math

import jax
import jax.numpy as jnp
from jax import lax
from jax.experimental import pallas as pl
from jax.experimental.pallas import tpu as pltpu

F32 = jnp.float32
BF16 = jnp.bfloat16

GRID_W = 64
D_HEAD = 128
H_DIFF = 8
DIFF_W = 2 * D_HEAD
FNET_GROUPS = 8
FNET_GROUP_W = 256
H_GQA = 16
H_KV = 4
GQA_GROUP = H_GQA // H_KV
GQA_W = GQA_GROUP * D_HEAD
BRANCH_W = 2048
N_BRANCH = 3
ROPE_THETA = 10000.0
NORM_EPS = 1e-6

OFF_AQ = 0
OFF_AK = OFF_AQ + H_DIFF * DIFF_W
OFF_AV = OFF_AK + H_DIFF * DIFF_W
OFF_B = OFF_AV + H_DIFF * DIFF_W
OFF_CQ = OFF_B + FNET_GROUPS * FNET_GROUP_W
OFF_CK = OFF_CQ + H_GQA * D_HEAD
OFF_CV = OFF_CK + H_KV * D_HEAD
D_IN = OFF_CV + H_KV * D_HEAD
Q_SCALE = D_HEAD ** -0.5 * math.log2(math.e)

V7X_VMEM_LIMIT_BYTES = 56 * 1024 * 1024
LANES = 128
IN_TILE = 512
ATTN_ROWS = 256


def _params(sem):
    return pltpu.CompilerParams(dimension_semantics=sem, vmem_limit_bytes=V7X_VMEM_LIMIT_BYTES)


def _pick(dim, pref, align):
    t = (min(pref, dim) // align) * align
    while t >= align:
        if dim % t == 0:
            return t
        t -= align
    return dim


def _dot(a, b):
    return jnp.dot(a, b, preferred_element_type=F32)


def _dot_nt(a, b):
    return lax.dot_general(a, b, (((1,), (1,)), ((), ())), preferred_element_type=F32)


def _cast_kernel(x_ref, o_ref):
    o_ref[...] = x_ref[...].astype(BF16)


def _cast(w):
    nl, r, c = w.shape
    br = _pick(r, 512, 8)
    bc = _pick(c, 2048, LANES)
    spec = pl.BlockSpec((1, br, bc), lambda l, i, j: (l, i, j))
    return pl.pallas_call(
        _cast_kernel, grid=(nl, r // br, c // bc), in_specs=[spec], out_specs=spec,
        out_shape=jax.ShapeDtypeStruct(w.shape, BF16),
        compiler_params=_params(("arbitrary",) * 3), name="cast_weight",
    )(w)


def _cast_pad_cols_kernel(x_ref, o_ref):
    c = x_ref.shape[-1]
    o_ref[:, :, :c] = x_ref[...].astype(BF16)
    if o_ref.shape[-1] > c:
        o_ref[:, :, c:] = jnp.zeros(o_ref.shape[:2] + (o_ref.shape[-1] - c,), BF16)


def _cast_pad_cols(w, nseg, cp):
    nl, r, c_all = w.shape
    c = c_all // nseg
    br = _pick(r, 64, 8)
    return pl.pallas_call(
        _cast_pad_cols_kernel, grid=(nl, r // br, nseg),
        in_specs=[pl.BlockSpec((1, br, c), lambda l, i, s: (l, i, s))],
        out_specs=pl.BlockSpec((1, br, cp), lambda l, i, s: (l, i, s)),
        out_shape=jax.ShapeDtypeStruct((nl, r, nseg * cp), BF16),
        compiler_params=_params(("arbitrary",) * 3), name="cast_pad_cols",
    )(w)


def _cast_pad_rows_kernel(x_ref, o_ref, *, n_valid):
    i = pl.program_id(1)
    o_ref[...] = jnp.where(i < n_valid, x_ref[...], 0.0).astype(BF16)


def _cast_pad_rows(w, rp):
    nl, r, c = w.shape
    br = math.gcd(r, rp)
    n_valid = r // br
    return pl.pallas_call(
        functools.partial(_cast_pad_rows_kernel, n_valid=n_valid), grid=(nl, rp // br),
        in_specs=[pl.BlockSpec((1, br, c), lambda l, i: (l, jnp.minimum(i, n_valid - 1), 0))],
        out_specs=pl.BlockSpec((1, br, c), lambda l, i: (l, i, 0)),
        out_shape=jax.ShapeDtypeStruct((nl, rp, c), BF16),
        compiler_params=_params(("arbitrary",) * 2), name="cast_pad_rows",
    )(w)


def _bf16_pieces(x, n):
    pieces = []
    for _ in range(n):
        p = x.astype(BF16).astype(F32)
        pieces.append(p)
        x = x - p
    return pieces


def _ada_kernel(c_ref, w_ref, b_ref, o_ref):
    c = c_ref[...]
    rows = c.shape[0]
    s3 = jnp.concatenate(_bf16_pieces(c * jax.nn.sigmoid(c), 3), axis=0).astype(BF16)
    w_hi, w_lo = _bf16_pieces(w_ref[0], 2)
    hi = _dot(s3, w_hi.astype(BF16))
    lo = _dot(s3, w_lo.astype(BF16))
    acc = (hi[:rows] + hi[rows:2 * rows] + hi[2 * rows:]) + (lo[:rows] + lo[rows:2 * rows])
    o_ref[0] = acc + b_ref[0]


def _ada(cvec, w_ada, b_ada):
    depth, d, n = w_ada.shape
    bn = _pick(n, 512, LANES)
    return pl.pallas_call(
        _ada_kernel,
        grid=(depth, n // bn),
        in_specs=[pl.BlockSpec((8, d), lambda l, j: (0, 0)),
                  pl.BlockSpec((1, d, bn), lambda l, j: (l, 0, j)),
                  pl.BlockSpec((1, 1, bn), lambda l, j: (l, 0, j))],
        out_specs=pl.BlockSpec((1, 8, bn), lambda l, j: (l, 0, j)),
        out_shape=jax.ShapeDtypeStruct((depth, 8, n), F32),
        compiler_params=_params(("arbitrary", "arbitrary")),
        name="ada",
    )(cvec, w_ada, b_ada.reshape(depth, 1, n))


def _mod_kernel(x_ref, sc_ref, sh_ref, h_ref):
    h_ref[...] = (x_ref[...] * (1.0 + sc_ref[0]) + sh_ref[0]).astype(BF16)


def _modulate(x, sc, sh, rows_per_group):
    m, d = x.shape
    bs = _pick(rows_per_group, 256, 8)
    grp = lambda i: (i * bs // rows_per_group, 0, 0)
    return pl.pallas_call(
        _mod_kernel,
        grid=(m // bs,),
        in_specs=[pl.BlockSpec((bs, d), lambda i: (i, 0)),
                  pl.BlockSpec((1, 1, d), grp),
                  pl.BlockSpec((1, 1, d), grp)],
        out_specs=pl.BlockSpec((bs, d), lambda i: (i, 0)),
        out_shape=jax.ShapeDtypeStruct((m, d), BF16),
        compiler_params=_params(("arbitrary",)),
        name="modulate",
    )(x, sc, sh)


def _layer_norm(y, g, b):
    mu = jnp.mean(y, axis=-1, keepdims=True)
    dlt = y - mu
    var = jnp.mean(dlt * dlt, axis=-1, keepdims=True)
    return dlt * lax.rsqrt(var + NORM_EPS) * g + b


def _ln_mod_kernel(y_ref, g_ref, b_ref, sc_ref, sh_ref, x_ref, h_ref):
    xn = _layer_norm(y_ref[...], g_ref[...], b_ref[...])
    x_ref[...] = xn
    h_ref[...] = (xn * (1.0 + sc_ref[0]) + sh_ref[0]).astype(BF16)


def _ln_kernel(y_ref, g_ref, b_ref, x_ref):
    x_ref[...] = _layer_norm(y_ref[...], g_ref[...], b_ref[...])


def _ln(y, g, b, mod=None, rows_per_group=None):
    m, d = y.shape
    bs = _pick(rows_per_group or m, 256, 8)
    row = pl.BlockSpec((bs, d), lambda i: (i, 0))
    vec = pl.BlockSpec((1, d), lambda i: (0, 0))
    g2, b2 = g.reshape(1, d), b.reshape(1, d)
    if mod is None:
        return pl.pallas_call(
            _ln_kernel, grid=(m // bs,), in_specs=[row, vec, vec], out_specs=row,
            out_shape=jax.ShapeDtypeStruct((m, d), F32),
            compiler_params=_params(("arbitrary",)), name="layernorm",
        )(y, g2, b2)
    grp = pl.BlockSpec((1, 1, d), lambda i: (i * bs // rows_per_group, 0, 0))
    return pl.pallas_call(
        _ln_mod_kernel, grid=(m // bs,), in_specs=[row, vec, vec, grp, grp],
        out_specs=[row, row],
        out_shape=[jax.ShapeDtypeStruct((m, d), F32), jax.ShapeDtypeStruct((m, d), BF16)],
        compiler_params=_params(("arbitrary",)), name="layernorm_mod",
    )(y, g2, b2, mod[0], mod[1])


def _mm_kernel(*refs, nk, n_extra, epilogue):
    x_ref, w_ref = refs[0], refs[1]
    extra = refs[2:2 + n_extra]
    o_ref = refs[2 + n_extra]
    part = _dot(x_ref[...], w_ref[0])
    if nk == 1:
        o_ref[...] = epilogue(part, *extra).astype(o_ref.dtype)
        return
    acc_ref = refs[3 + n_extra]
    k = pl.program_id(2)

    @pl.when(k == 0)
    def _():
        acc_ref[...] = part

    @pl.when((k > 0) & (k < nk - 1))
    def _():
        acc_ref[...] += part

    @pl.when(k == nk - 1)
    def _():
        o_ref[...] = epilogue(acc_ref[...] + part, *extra).astype(o_ref.dtype)


def _matmul(x, w, layer, *, bm, bn, bk, out_dtype, epilogue, extra=(), extra_specs=(), name):
    m, kdim = x.shape
    n = w.shape[2]
    nk = kdim // bk
    kernel = functools.partial(_mm_kernel, nk=nk, n_extra=len(extra), epilogue=epilogue)
    return pl.pallas_call(
        kernel,
        grid=(m // bm, n // bn, nk),
        in_specs=[pl.BlockSpec((bm, bk), lambda i, j, k: (i, k)),
                  pl.BlockSpec((1, bk, bn), lambda i, j, k: (layer, k, j)),
                  *extra_specs],
        out_specs=pl.BlockSpec((bm, bn), lambda i, j, k: (i, j)),
        out_shape=jax.ShapeDtypeStruct((m, n), out_dtype),
        scratch_shapes=[pltpu.VMEM((bm, bn), F32)] if nk > 1 else [],
        compiler_params=_params(("arbitrary", "arbitrary", "arbitrary")),
        name=name,
    )(x, w, *extra)


def _identity(acc):
    return acc


def _residual_epilogue(acc, x_ref, gate_ref, *, alpha):
    return alpha * x_ref[...] + gate_ref[0] * acc


def _matmul_residual(a, w, layer, xres, gate, rows_per_group, alpha, *, bk, name):
    n = w.shape[2]
    bm = _pick(rows_per_group, 1024, 8)
    bn = _pick(n, 1024, LANES)
    return _matmul(
        a, w, layer, bm=bm, bn=bn, bk=bk, out_dtype=F32,
        epilogue=functools.partial(_residual_epilogue, alpha=alpha),
        extra=(xres, gate),
        extra_specs=(pl.BlockSpec((bm, bn), lambda i, j, k: (i, j)),
                     pl.BlockSpec((1, 1, bn), lambda i, j, k: (i * bm // rows_per_group, 0, j))),
        name=name)


def _rope(x, cos, sin_signed):
    return x * cos + pltpu.roll(x, D_HEAD // 2, 1) * sin_signed


def _inproj_kernel(h_ref, w_ref, cos_ref, sin_ref, g_ref, o_ref):
    j = pl.program_id(1)
    acc = _dot(h_ref[...], w_ref[0])
    n_chunks = IN_TILE // D_HEAD
    t_ak, t_av, t_cq, t_ck, t_cv = (OFF_AK // IN_TILE, OFF_AV // IN_TILE, OFF_CQ // IN_TILE,
                                    OFF_CK // IN_TILE, OFF_CV // IN_TILE)

    @pl.when(j < t_av)
    def _():
        s = jnp.where(j < t_ak, Q_SCALE, 1.0).astype(F32)
        cos = cos_ref[...] * s
        sin = sin_ref[...] * s
        for c in range(n_chunks):
            sl = slice(c * D_HEAD, (c + 1) * D_HEAD)
            o_ref[:, sl] = _rope(acc[:, sl], cos, sin).astype(BF16)

    @pl.when(((j >= t_av) & (j < t_cq)) | (j >= t_cv))
    def _():
        o_ref[...] = acc.astype(BF16)

    @pl.when((j >= t_cq) & (j < t_cv))
    def _():
        is_q = j < t_ck
        g = jnp.where(is_q, g_ref[0:1, :], g_ref[1:2, :])
        s = jnp.where(is_q, Q_SCALE, 1.0).astype(F32)
        cos = cos_ref[...] * s
        sin = sin_ref[...] * s
        for c in range(n_chunks):
            sl = slice(c * D_HEAD, (c + 1) * D_HEAD)
            x = acc[:, sl]
            ms = jnp.mean(x * x, axis=-1, keepdims=True)
            xn = x * lax.rsqrt(ms + NORM_EPS) * g
            o_ref[:, sl] = _rope(xn, cos, sin).astype(BF16)


def _inproj(h, w, layer, cos, sin_signed, qk_g, rows_per_seq):
    m, d = h.shape
    bm = _pick(rows_per_seq, 1024, 8)
    nseq_blk = rows_per_seq // bm
    tab = pl.BlockSpec((bm, D_HEAD), lambda i, j: (i % nseq_blk, 0))
    return pl.pallas_call(
        _inproj_kernel,
        grid=(m // bm, D_IN // IN_TILE),
        in_specs=[pl.BlockSpec((bm, d), lambda i, j: (i, 0)),
                  pl.BlockSpec((1, d, IN_TILE), lambda i, j: (layer, 0, j)),
                  tab, tab,
                  pl.BlockSpec((2, D_HEAD), lambda i, j: (0, 0))],
        out_specs=pl.BlockSpec((bm, IN_TILE), lambda i, j: (i, j)),
        out_shape=jax.ShapeDtypeStruct((m, D_IN), BF16),
        compiler_params=_params(("arbitrary", "arbitrary")),
        name="inproj",
    )(h, w, cos, sin_signed, qk_g)


def _scores(q, k_refs, lo):
    return [_dot_nt(q, k_ref[:, lo:lo + D_HEAD]) for k_ref in k_refs]


def _softmax_parts(scores):
    mx = functools.reduce(jnp.maximum, [jnp.max(s, axis=-1, keepdims=True) for s in scores])
    es = [jnp.exp2(s - mx) for s in scores]
    den = functools.reduce(jnp.add, [jnp.sum(e, axis=-1, keepdims=True) for e in es])
    return es, den


def _pipelined(n_chains, issue, finish):
    nxt = issue(0)
    for c in range(n_chains):
        cur = nxt
        if c + 1 < n_chains:
            nxt = issue(c + 1)
        finish(c, cur)


def _diff_attn_kernel(*refs, n_src, lam_init):
    q_ref = refs[0]
    k_refs = refs[1:1 + n_src]
    v_refs = refs[1 + n_src:1 + 2 * n_src]
    lam_ref, g_ref, o_ref = refs[1 + 2 * n_src:]
    lp = lam_ref[...]
    lam = (jnp.exp(jnp.sum(lp[0:1] * lp[1:2], axis=-1, keepdims=True))
           - jnp.exp(jnp.sum(lp[2:3] * lp[3:4], axis=-1, keepdims=True)) + lam_init)
    rows = min(ATTN_ROWS, q_ref.shape[0])

    def issue(c):
        q = q_ref[c * rows:(c + 1) * rows, :]
        return _scores(q[:, :D_HEAD], k_refs, 0), _scores(q[:, D_HEAD:], k_refs, D_HEAD)

    def finish(c, scores):
        e1, den1 = _softmax_parts(scores[0])
        e2, den2 = _softmax_parts(scores[1])
        a1 = 1.0 / den1
        a2 = lam / den2
        o = None
        for x1, x2, v_ref in zip(e1, e2, v_refs):
            p = (x1 * a1 - x2 * a2).astype(BF16)
            pv = _dot(p, v_ref[...])
            o = pv if o is None else o + pv
        ms = jnp.mean(o * o, axis=-1, keepdims=True)
        y = (o * lax.rsqrt(ms + NORM_EPS)) * g_ref[...] * (1.0 - lam_init)
        o_ref[c * rows:(c + 1) * rows, :] = y.astype(BF16)

    _pipelined(q_ref.shape[0] // rows, issue, finish)


def _diff_attention(pq, kv_sources, lam_p, subln_g, lam_init, batch, q_len, bq):
    nq = q_len // bq
    kblk, vblk = OFF_AK // DIFF_W, OFF_AV // DIFF_W
    k_specs = [pl.BlockSpec((n, DIFF_W), lambda b, h, i: (b, kblk + h)) for _, n in kv_sources]
    v_specs = [pl.BlockSpec((n, DIFF_W), lambda b, h, i: (b, vblk + h)) for _, n in kv_sources]
    srcs = [p for p, _ in kv_sources]
    kernel = functools.partial(_diff_attn_kernel, n_src=len(srcs), lam_init=lam_init)
    return pl.pallas_call(
        kernel,
        grid=(batch, H_DIFF, nq),
        in_specs=[pl.BlockSpec((bq, DIFF_W), lambda b, h, i: (b * nq + i, h)),
                  *k_specs, *v_specs,
                  pl.BlockSpec((4, D_HEAD), lambda b, h, i: (0, 0)),
                  pl.BlockSpec((1, DIFF_W), lambda b, h, i: (0, 0))],
        out_specs=pl.BlockSpec((bq, DIFF_W), lambda b, h, i: (b * nq + i, h)),
        out_shape=jax.ShapeDtypeStruct((batch * q_len, H_DIFF * DIFF_W), BF16),
        compiler_params=_params(("arbitrary", "arbitrary", "arbitrary")),
        name="diff_attention",
    )(pq, *srcs, *srcs, lam_p, subln_g.reshape(1, DIFF_W))


def _gqa_kernel(*refs, n_src):
    q_ref = refs[0]
    k_refs = refs[1:1 + n_src]
    v_refs = refs[1 + n_src:1 + 2 * n_src]
    o_ref = refs[1 + 2 * n_src]
    rows = min(ATTN_ROWS, q_ref.shape[0])
    n_row_blocks = q_ref.shape[0] // rows

    def window(c):
        g, r = divmod(c, n_row_blocks)
        return slice(r * rows, (r + 1) * rows), slice(g * D_HEAD, (g + 1) * D_HEAD)

    def issue(c):
        rs, cs = window(c)
        return _scores(q_ref[rs, cs], k_refs, 0)

    def finish(c, scores):
        rs, cs = window(c)
        es, den = _softmax_parts(scores)
        o = None
        for e, v_ref in zip(es, v_refs):
            pv = _dot(e.astype(BF16), v_ref[...])
            o = pv if o is None else o + pv
        o_ref[rs, cs] = (o / den).astype(BF16)

    _pipelined(GQA_GROUP * n_row_blocks, issue, finish)


def _gqa_attention(pq, kv_sources, batch, q_len, bq):
    nq = q_len // bq
    qblk, kblk, vblk = OFF_CQ // GQA_W, OFF_CK // D_HEAD, OFF_CV // D_HEAD
    k_specs = [pl.BlockSpec((n, D_HEAD), lambda b, kh, i: (b, kblk + kh)) for _, n in kv_sources]
    v_specs = [pl.BlockSpec((n, D_HEAD), lambda b, kh, i: (b, vblk + kh)) for _, n in kv_sources]
    srcs = [p for p, _ in kv_sources]
    return pl.pallas_call(
        functools.partial(_gqa_kernel, n_src=len(srcs)),
        grid=(batch, H_KV, nq),
        in_specs=[pl.BlockSpec((bq, GQA_W), lambda b, kh, i: (b * nq + i, qblk + kh)),
                  *k_specs, *v_specs],
        out_specs=pl.BlockSpec((bq, GQA_W), lambda b, kh, i: (b * nq + i, kh)),
        out_shape=jax.ShapeDtypeStruct((batch * q_len, H_GQA * D_HEAD), BF16),
        compiler_params=_params(("arbitrary",) * 3),
        name="gqa_attention",
    )(pq, *srcs, *srcs)


def _dft_tables(n, split=64):
    k = jnp.arange(n, dtype=jnp.int32)

    def cos_sin(t):
        ang = ((k[:, None] * t[None, :]) % n).astype(F32) * (2.0 * math.pi / n)
        return jnp.cos(ang), jnp.sin(ang)

    if n <= split or n % split:
        c, s = cos_sin(k)
    else:
        ca, sa = (v[:, :, None] for v in cos_sin(jnp.arange(0, n, split, dtype=jnp.int32)))
        cb, sb = (v[:, None, :] for v in cos_sin(jnp.arange(split, dtype=jnp.int32)))
        c = (ca * cb - sa * sb).reshape(n, n)
        s = (sa * cb + ca * sb).reshape(n, n)
    return c.astype(BF16), s.astype(BF16)


def _fnet_kernel(cn_ref, sn_ref, u_ref, cc_ref, sc_ref, o_ref, *, scale):
    u = u_ref[...]
    t1 = _dot(cn_ref[...], u).astype(BF16)
    t2 = _dot(sn_ref[...], u).astype(BF16)
    for g in range(u.shape[1] // FNET_GROUP_W):
        sl = slice(g * FNET_GROUP_W, (g + 1) * FNET_GROUP_W)
        y = _dot(t1[:, sl], cc_ref[...]) - _dot(t2[:, sl], sc_ref[...])
        o_ref[:, sl] = (y * scale).astype(BF16)


def _fourier_mix(p, batch, n, pos_tabs, chan_tabs):
    bm = _pick(n, 1024, 8)
    bc = 2 * FNET_GROUP_W
    ni = n // bm
    ublk = OFF_B // bc
    pos = pl.BlockSpec((bm, n), lambda b, i, j: (i, 0))
    chan = pl.BlockSpec((FNET_GROUP_W, FNET_GROUP_W), lambda b, i, j: (0, 0))
    scale = 1.0 / math.sqrt(n * FNET_GROUP_W)
    return pl.pallas_call(
        functools.partial(_fnet_kernel, scale=scale),
        grid=(batch, ni, FNET_GROUPS * FNET_GROUP_W // bc),
        in_specs=[pos, pos, pl.BlockSpec((n, bc), lambda b, i, j: (b, ublk + j)), chan, chan],
        out_specs=pl.BlockSpec((bm, bc), lambda b, i, j: (b * ni + i, j)),
        out_shape=jax.ShapeDtypeStruct((batch * n, FNET_GROUPS * FNET_GROUP_W), BF16),
        compiler_params=_params(("arbitrary",) * 3),
        name="fourier_mix",
    )(*pos_tabs, p, *chan_tabs)


def _merge_kernel(h_ref, oa_ref, ob_ref, oc_ref, wg_ref, bg_ref, wb_ref, o_ref, acc_ref):
    n = pl.program_id(2)
    gate = jax.nn.sigmoid(_dot(h_ref[...], wg_ref[0]) + bg_ref[0])

    @pl.when(n == 0)
    def _():
        acc_ref[...] = gate * _dot(oa_ref[...], wb_ref[0])

    @pl.when(n == 1)
    def _():
        acc_ref[...] += gate * _dot(ob_ref[...], wb_ref[0])

    @pl.when(n == 2)
    def _():
        o_ref[...] = (acc_ref[...] + gate * _dot(oc_ref[...], wb_ref[0])).astype(BF16)


def _gated_merge(h, oa, ob, oc, w_gate, b_gate, w_branch, layer):
    m, d = h.shape
    bm = _pick(m, 1024, 8)
    bn = _pick(d, 512, LANES)
    nj = d // bn
    once = pl.Buffered(1)
    br = pl.BlockSpec((bm, BRANCH_W), lambda i, j, n: (i, 0), pipeline_mode=once)
    return pl.pallas_call(
        _merge_kernel,
        grid=(m // bm, nj, N_BRANCH),
        in_specs=[pl.BlockSpec((bm, d), lambda i, j, n: (i, 0), pipeline_mode=once), br, br, br,
                  pl.BlockSpec((1, d, bn), lambda i, j, n: (layer, 0, n * nj + j)),
                  pl.BlockSpec((1, 1, bn), lambda i, j, n: (layer, 0, n * nj + j)),
                  pl.BlockSpec((1, BRANCH_W, bn), lambda i, j, n: (layer * N_BRANCH + n, 0, j))],
        out_specs=pl.BlockSpec((bm, bn), lambda i, j, n: (i, j)),
        out_shape=jax.ShapeDtypeStruct((m, d), BF16),
        scratch_shapes=[pltpu.VMEM((bm, bn), F32)],
        compiler_params=_params(("arbitrary",) * 3),
        name="gated_merge",
    )(h, oa, ob, oc, w_gate, b_gate, w_branch)


def _conv_act_kernel(a_ref, g_ref, w_ref, b_ref, o_ref):
    a = a_ref[0]
    s = a.shape[0]
    row = lax.broadcasted_iota(jnp.int32, a.shape, 0)
    prev = jnp.where(row == 0, 0.0, pltpu.roll(a, 1, 0))
    nxt = jnp.where(row == s - 1, 0.0, pltpu.roll(a, s - 1, 0))
    w = w_ref[...]
    z = prev * w[0:1] + a * w[1:2] + nxt * w[2:3] + b_ref[...]
    gelu = 0.5 * z * (1.0 + lax.erf(z * math.sqrt(0.5)))
    o_ref[0] = (gelu * g_ref[0]).astype(BF16)


def _conv_act(u, conv_w, conv_b, batch, s):
    ffp = conv_w.shape[1]
    bc = _pick(ffp, 256, LANES)
    nj = ffp // bc
    u3 = u.reshape(batch, s, 2 * ffp)
    out = pl.pallas_call(
        _conv_act_kernel,
        grid=(batch, nj),
        in_specs=[pl.BlockSpec((1, s, bc), lambda b, j: (b, 0, j)),
                  pl.BlockSpec((1, s, bc), lambda b, j: (b, 0, nj + j)),
                  pl.BlockSpec((3, bc), lambda b, j: (0, j)),
                  pl.BlockSpec((1, bc), lambda b, j: (0, j))],
        out_specs=pl.BlockSpec((1, s, bc), lambda b, j: (b, 0, j)),
        out_shape=jax.ShapeDtypeStruct((batch, s, ffp), BF16),
        compiler_params=_params(("arbitrary", "arbitrary")),
        name="conv_act",
    )(u3, u3, conv_w, conv_b)
    return out.reshape(batch * s, ffp)


def _rope_tables(n):
    n_rows = n // GRID_W
    rows = jnp.broadcast_to(jnp.arange(n_rows)[:, None], (n_rows, GRID_W)).reshape(-1).astype(F32)
    cols = jnp.broadcast_to(jnp.arange(GRID_W)[None, :], (n_rows, GRID_W)).reshape(-1).astype(F32)
    quarter = D_HEAD // 4
    inv_freq = ROPE_THETA ** (-jnp.arange(quarter, dtype=F32) / quarter)
    ang = jnp.concatenate([rows[:, None] * inv_freq, cols[:, None] * inv_freq], axis=-1)
    ang = jnp.concatenate([ang, ang], axis=-1)
    sign = jnp.where(jnp.arange(D_HEAD) < D_HEAD // 2, -1.0, 1.0).astype(F32)
    return jnp.cos(ang), jnp.sin(ang) * sign


def kernel(x, c, ctx, c_ctx, w_ada, b_ada, w_in, diff_lambda, diff_subln_g, qk_norm_g, w_branch, w_gate,
           b_gate, w_o, ln1_g, ln1_b, w_up, conv_w, conv_b, w_down, ln2_g, ln2_b):
    batch, seq, d = x.shape
    n_ctx = ctx.shape[1]
    depth = w_ada.shape[0]
    ff = conv_w.shape[2]
    assert ff % LANES == 0 and w_in.shape[2] == D_IN
    ffp = -(-ff // 1024) * 1024 if ff > 1024 else ff
    alpha = (2 * depth) ** 0.25

    cvec = jnp.zeros((8, d), F32).at[:batch].set(c).at[batch].set(c_ctx)
    mod = _ada(cvec, w_ada, b_ada).reshape(depth, 8, 6, d)

    w_in_b = _cast(w_in)
    w_gate_b = _cast(w_gate)
    w_branch_b = _cast(w_branch.reshape(depth * N_BRANCH, BRANCH_W, d))
    w_o_b = _cast(w_o)
    w_up_b = _cast_pad_cols(w_up, 2, ffp)
    w_down_b = _cast_pad_rows(w_down, ffp)
    pad = ffp - ff
    conv_w_p = jnp.pad(conv_w, ((0, 0), (0, 0), (0, pad)))
    conv_b_p = jnp.pad(conv_b, ((0, 0), (0, pad))).reshape(depth, 1, ffp)
    b_gate3 = b_gate.reshape(depth, 1, N_BRANCH * d)

    cos_l, sin_l = _rope_tables(seq)
    cos_c, sin_c = jnp.ones((n_ctx, D_HEAD), F32), jnp.zeros((n_ctx, D_HEAD), F32)
    pos_l, pos_c = _dft_tables(seq), _dft_tables(n_ctx)
    chan = _dft_tables(FNET_GROUP_W)

    xl = x.reshape(batch * seq, d)
    xc = ctx.reshape(batch * n_ctx, d)
    m_ctx = batch * n_ctx
    bq_diff_l = _pick(seq, 4 * ATTN_ROWS, 8)
    bq_gqa_l = _pick(seq, 2 * ATTN_ROWS, 8)
    bq_c = _pick(n_ctx, ATTN_ROWS, 8)
    bk_ff = _pick(ffp, 2816, LANES)

    def lat(i, l):
        return mod[l, :batch, i][:, None, :]

    def cx(i, l):
        return mod[l, batch, i][None, None, :]

    hl = _modulate(xl, lat(1, 0), lat(0, 0), seq)
    hc = _modulate(xc, cx(1, 0), cx(0, 0), m_ctx)

    for l in range(depth):
        last = l == depth - 1
        lam_init = 0.8 - 0.6 * math.exp(-0.3 * l)

        pl_ = _inproj(hl, w_in_b, l, cos_l, sin_l, qk_norm_g[l], seq)
        pc_ = _inproj(hc, w_in_b, l, cos_c, sin_c, qk_norm_g[l], n_ctx)
        kv = [(pl_, seq), (pc_, n_ctx)]
        oa = _diff_attention(pl_, kv, diff_lambda[l], diff_subln_g[l], lam_init, batch, seq, bq_diff_l)
        ob = _fourier_mix(pl_, batch, seq, pos_l, chan)
        oc = _gqa_attention(pl_, kv, batch, seq, bq_gqa_l)
        merged = _gated_merge(hl, oa, ob, oc, w_gate_b, b_gate3, w_branch_b, l)
        yl = _matmul_residual(merged, w_o_b, l, xl, lat(2, l), seq, alpha, bk=d, name="out_proj")
        xl, hl = _ln(yl, ln1_g[l], ln1_b[l], mod=(lat(4, l), lat(3, l)), rows_per_group=seq)

        if not last:
            kvc = [(pc_, n_ctx)]
            oa_c = _diff_attention(pc_, kvc, diff_lambda[l], diff_subln_g[l], lam_init, batch, n_ctx, bq_c)
            ob_c = _fourier_mix(pc_, batch, n_ctx, pos_c, chan)
            oc_c = _gqa_attention(pc_, kvc, batch, n_ctx, bq_c)
            merged_c = _gated_merge(hc, oa_c, ob_c, oc_c, w_gate_b, b_gate3, w_branch_b, l)
            yc = _matmul_residual(merged_c, w_o_b, l, xc, cx(2, l), m_ctx, alpha, bk=d, name="out_proj_ctx")
            xc, hc = _ln(yc, ln1_g[l], ln1_b[l], mod=(cx(4, l), cx(3, l)), rows_per_group=m_ctx)

        def ffn(h, xres, gate, rows_per_group, s):
            mrows = h.shape[0]
            u = _matmul(h, w_up_b, l, bm=_pick(rows_per_group, 1024, 8), bn=_pick(2 * ffp, 1024, LANES), bk=d,
                        out_dtype=F32, epilogue=_identity, name="ffn_up")
            act = _conv_act(u, conv_w_p[l], conv_b_p[l], mrows // s, s)
            return _matmul_residual(act, w_down_b, l, xres, gate, rows_per_group, alpha, bk=bk_ff,
                                    name="ffn_down")

        yl = ffn(hl, xl, lat(5, l), seq, seq)
        if last:
            xl = _ln(yl, ln2_g[l], ln2_b[l])
        else:
            xl, hl = _ln(yl, ln2_g[l], ln2_b[l], mod=(lat(1, l + 1), lat(0, l + 1)), rows_per_group=seq)
            yc = ffn(hc, xc, cx(5, l), m_ctx, n_ctx)
            xc, hc = _ln(yc, ln2_g[l], ln2_b[l], mod=(cx(1, l + 1), cx(0, l + 1)), rows_per_group=m_ctx)

    return xl.reshape(batch, seq, d)
```

```python
import functools
import math

import jax
import jax.numpy as jnp
from jax import lax
from jax.experimental import pallas as pl
from jax.experimental.pallas import tpu as pltpu

F32 = jnp.float32
BF16 = jnp.bfloat16

GRID_W = 64
D_HEAD = 128
H_DIFF = 8
DIFF_W = 2 * D_HEAD
FNET_GROUPS = 8
FNET_GROUP_W = 256
H_GQA = 16
H_KV = 4
GQA_GROUP = H_GQA // H_KV
GQA_W = GQA_GROUP * D_HEAD
BRANCH_W = 2048
N_BRANCH = 3
ROPE_THETA = 10000.0
NORM_EPS = 1e-6

OFF_AQ = 0
OFF_AK = OFF_AQ + H_DIFF * DIFF_W
OFF_AV = OFF_AK + H_DIFF * DIFF_W
OFF_B = OFF_AV + H_DIFF * DIFF_W
OFF_CQ = OFF_B + FNET_GROUPS * FNET_GROUP_W
OFF_CK = OFF_CQ + H_GQA * D_HEAD
OFF_CV = OFF_CK + H_KV * D_HEAD
D_IN = OFF_CV + H_KV * D_HEAD
Q_SCALE = D_HEAD ** -0.5 * math.log2(math.e)

V7X_VMEM_LIMIT_BYTES = 56 * 1024 * 1024
LANES = 128
IN_TILE = 512
ATTN_ROWS = 256


def _params(sem):
    return pltpu.CompilerParams(dimension_semantics=sem, vmem_limit_bytes=V7X_VMEM_LIMIT_BYTES)


def _pick(dim, pref, align):
    t = (min(pref, dim) // align) * align
    while t >= align:
        if dim % t == 0:
            return t
        t -= align
    return dim


def _dot(a, b):
    return jnp.dot(a, b, preferred_element_type=F32)


def _dot_nt(a, b):
    return lax.dot_general(a, b, (((1,), (1,)), ((), ())), preferred_element_type=F32)


def _pipelined(n_chains, issue, finish, lookahead=1):
    pending = [issue(c) for c in range(min(lookahead, n_chains))]
    for c in range(n_chains):
        if c + lookahead < n_chains:
            pending.append(issue(c + lookahead))
        finish(c, pending.pop(0))


class _Cast:
    def __init__(self, w, layer, max_blocks, *, nseg=1, cp=None, rp=None):
        _, r, c_all = w.shape
        self.w, self.layer, self.nseg = w, layer, nseg
        self.c = c_all // nseg
        self.cp = cp or self.c
        rows_out = rp or r
        div = math.gcd(r, rows_out)
        self.rb = next(rb for rb in range(16, div + 1, 16) if div % rb == 0 and rows_out // rb <= max_blocks)
        self.n_in, self.n_out = r // self.rb, rows_out // self.rb
        self.out_shape = jax.ShapeDtypeStruct((1, rows_out, nseg * self.cp), BF16)

    def specs(self, step):
        out_blk = lambda *g: jnp.minimum(step(*g), self.n_out - 1)
        return (pl.BlockSpec((1, self.rb, self.nseg * self.c),
                             lambda *g: (self.layer, jnp.minimum(out_blk(*g), self.n_in - 1), 0)),
                pl.BlockSpec((1, self.rb, self.nseg * self.cp), lambda *g: (0, out_blk(*g), 0)))

    def body(self, t, x_ref, o_ref):
        x = x_ref[0]
        if self.n_out > self.n_in:
            x = jnp.where(jnp.minimum(t, self.n_out - 1) < self.n_in, x, 0.0)
        for s in range(self.nseg):
            o_ref[0, :, s * self.cp:s * self.cp + self.c] = x[:, s * self.c:(s + 1) * self.c].astype(BF16)
            if self.cp > self.c:
                o_ref[0, :, s * self.cp + self.c:(s + 1) * self.cp] = jnp.zeros((self.rb, self.cp - self.c), BF16)


def _run_casts(t, casts, refs):
    for n, cast in enumerate(casts):
        cast.body(t, refs[n], refs[len(casts) + n])


def _cast_kernel(x_ref, o_ref, *, cast):
    cast.body(pl.program_id(0), x_ref, o_ref)


def _cast_now(cast):
    in_spec, out_spec = cast.specs(lambda t: t)
    return pl.pallas_call(
        functools.partial(_cast_kernel, cast=cast), grid=(cast.n_out,), in_specs=[in_spec],
        out_specs=out_spec, out_shape=cast.out_shape,
        compiler_params=_params(("arbitrary",)), name="cast_weight",
    )(cast.w)


def _bf16_pieces(x, n):
    pieces = []
    for _ in range(n):
        p = x.astype(BF16).astype(F32)
        pieces.append(p)
        x = x - p
    return pieces


def _ada_kernel(c_ref, w_ref, b_ref, o_ref):
    c = c_ref[...]
    rows = c.shape[0]
    s3 = jnp.concatenate(_bf16_pieces(c * jax.nn.sigmoid(c), 3), axis=0).astype(BF16)
    w_hi, w_lo = _bf16_pieces(w_ref[0], 2)
    hi = _dot(s3, w_hi.astype(BF16))
    lo = _dot(s3, w_lo.astype(BF16))
    acc = (hi[:rows] + hi[rows:2 * rows] + hi[2 * rows:]) + (lo[:rows] + lo[rows:2 * rows])
    o_ref[0] = acc + b_ref[0]


def _ada(cvec, w_ada, b_ada):
    depth, d, n = w_ada.shape
    bn = _pick(n, 512, LANES)
    return pl.pallas_call(
        _ada_kernel,
        grid=(depth, n // bn),
        in_specs=[pl.BlockSpec((8, d), lambda l, j: (0, 0)),
                  pl.BlockSpec((1, d, bn), lambda l, j: (l, 0, j)),
                  pl.BlockSpec((1, 1, bn), lambda l, j: (l, 0, j))],
        out_specs=pl.BlockSpec((1, 8, bn), lambda l, j: (l, 0, j)),
        out_shape=jax.ShapeDtypeStruct((depth, 8, n), F32),
        compiler_params=_params(("arbitrary", "arbitrary")),
        name="ada",
    )(cvec, w_ada, b_ada.reshape(depth, 1, n))


def _mod_kernel(x_ref, sc_ref, sh_ref, h_ref):
    h_ref[...] = (x_ref[...] * (1.0 + sc_ref[0]) + sh_ref[0]).astype(BF16)


def _modulate(x, sc, sh, rows_per_group):
    m, d = x.shape
    bs = _pick(rows_per_group, 256, 8)
    grp = lambda i: (i * bs // rows_per_group, 0, 0)
    return pl.pallas_call(
        _mod_kernel,
        grid=(m // bs,),
        in_specs=[pl.BlockSpec((bs, d), lambda i: (i, 0)),
                  pl.BlockSpec((1, 1, d), grp),
                  pl.BlockSpec((1, 1, d), grp)],
        out_specs=pl.BlockSpec((bs, d), lambda i: (i, 0)),
        out_shape=jax.ShapeDtypeStruct((m, d), BF16),
        compiler_params=_params(("arbitrary",)),
        name="modulate",
    )(x, sc, sh)


def _layer_norm(y, g, b):
    mu = jnp.mean(y, axis=-1, keepdims=True)
    dlt = y - mu
    var = jnp.mean(dlt * dlt, axis=-1, keepdims=True)
    return dlt * lax.rsqrt(var + NORM_EPS) * g + b


def _ln_mod_kernel(y_ref, g_ref, b_ref, sc_ref, sh_ref, x_ref, h_ref):
    xn = _layer_norm(y_ref[...], g_ref[...], b_ref[...])
    x_ref[...] = xn
    h_ref[...] = (xn * (1.0 + sc_ref[0]) + sh_ref[0]).astype(BF16)


def _ln_kernel(y_ref, g_ref, b_ref, x_ref):
    x_ref[...] = _layer_norm(y_ref[...], g_ref[...], b_ref[...])


def _ln(y, g, b, mod=None, rows_per_group=None):
    m, d = y.shape
    bs = _pick(rows_per_group or m, 256, 8)
    row = pl.BlockSpec((bs, d), lambda i: (i, 0))
    vec = pl.BlockSpec((1, d), lambda i: (0, 0))
    g2, b2 = g.reshape(1, d), b.reshape(1, d)
    if mod is None:
        return pl.pallas_call(
            _ln_kernel, grid=(m // bs,), in_specs=[row, vec, vec], out_specs=row,
            out_shape=jax.ShapeDtypeStruct((m, d), F32),
            compiler_params=_params(("arbitrary",)), name="layernorm",
        )(y, g2, b2)
    grp = pl.BlockSpec((1, 1, d), lambda i: (i * bs // rows_per_group, 0, 0))
    return pl.pallas_call(
        _ln_mod_kernel, grid=(m // bs,), in_specs=[row, vec, vec, grp, grp],
        out_specs=[row, row],
        out_shape=[jax.ShapeDtypeStruct((m, d), F32), jax.ShapeDtypeStruct((m, d), BF16)],
        compiler_params=_params(("arbitrary",)), name="layernorm_mod",
    )(y, g2, b2, mod[0], mod[1])


def _mm_kernel(*refs, nk, n_extra, epilogue):
    x_ref, w_ref = refs[0], refs[1]
    extra = refs[2:2 + n_extra]
    o_ref = refs[2 + n_extra]
    part = _dot(x_ref[...], w_ref[0])
    if nk == 1:
        o_ref[...] = epilogue(part, *extra).astype(o_ref.dtype)
        return
    acc_ref = refs[3 + n_extra]
    k = pl.program_id(2)

    @pl.when(k == 0)
    def _():
        acc_ref[...] = part

    @pl.when((k > 0) & (k < nk - 1))
    def _():
        acc_ref[...] += part

    @pl.when(k == nk - 1)
    def _():
        o_ref[...] = epilogue(acc_ref[...] + part, *extra).astype(o_ref.dtype)


def _matmul(x, w, *, bm, bn, bk, out_dtype, epilogue, extra=(), extra_specs=(), name):
    m, kdim = x.shape
    n = w.shape[2]
    nk = kdim // bk
    kernel = functools.partial(_mm_kernel, nk=nk, n_extra=len(extra), epilogue=epilogue)
    return pl.pallas_call(
        kernel,
        grid=(m // bm, n // bn, nk),
        in_specs=[pl.BlockSpec((bm, bk), lambda i, j, k: (i, k)),
                  pl.BlockSpec((1, bk, bn), lambda i, j, k: (0, k, j)),
                  *extra_specs],
        out_specs=pl.BlockSpec((bm, bn), lambda i, j, k: (i, j)),
        out_shape=jax.ShapeDtypeStruct((m, n), out_dtype),
        scratch_shapes=[pltpu.VMEM((bm, bn), F32)] if nk > 1 else [],
        compiler_params=_params(("arbitrary", "arbitrary", "arbitrary")),
        name=name,
    )(x, w, *extra)


def _identity(acc):
    return acc


def _residual_epilogue(acc, x_ref, gate_ref, *, alpha):
    return alpha * x_ref[...] + gate_ref[0] * acc


def _matmul_residual(a, w, xres, gate, rows_per_group, alpha, *, bk, name):
    n = w.shape[2]
    bm = _pick(rows_per_group, 1024, 8)
    bn = _pick(n, 1024, LANES)
    return _matmul(
        a, w, bm=bm, bn=bn, bk=bk, out_dtype=F32,
        epilogue=functools.partial(_residual_epilogue, alpha=alpha),
        extra=(xres, gate),
        extra_specs=(pl.BlockSpec((bm, bn), lambda i, j, k: (i, j)),
                     pl.BlockSpec((1, 1, bn), lambda i, j, k: (i * bm // rows_per_group, 0, j))),
        name=name)


def _rope(x, cos, sin_signed):
    return x * cos + pltpu.roll(x, D_HEAD // 2, 1) * sin_signed


def _inproj_kernel(h_ref, w_ref, cos_ref, sin_ref, g_ref, *refs, casts, n_col_tiles):
    o_ref = refs[len(casts)]
    i, j = pl.program_id(0), pl.program_id(1)
    _run_casts(i * n_col_tiles + j, casts, refs[:len(casts)] + refs[len(casts) + 1:])
    acc = _dot(h_ref[...], w_ref[0])
    n_chunks = IN_TILE // D_HEAD
    t_ak, t_av, t_cq, t_ck, t_cv = (OFF_AK // IN_TILE, OFF_AV // IN_TILE, OFF_CQ // IN_TILE,
                                    OFF_CK // IN_TILE, OFF_CV // IN_TILE)

    @pl.when(j < t_av)
    def _():
        s = jnp.where(j < t_ak, Q_SCALE, 1.0).astype(F32)
        cos = cos_ref[...] * s
        sin = sin_ref[...] * s
        for c in range(n_chunks):
            sl = slice(c * D_HEAD, (c + 1) * D_HEAD)
            o_ref[:, sl] = _rope(acc[:, sl], cos, sin).astype(BF16)

    @pl.when(((j >= t_av) & (j < t_cq)) | (j >= t_cv))
    def _():
        o_ref[...] = acc.astype(BF16)

    @pl.when((j >= t_cq) & (j < t_cv))
    def _():
        is_q = j < t_ck
        g = jnp.where(is_q, g_ref[0:1, :], g_ref[1:2, :])
        s = jnp.where(is_q, Q_SCALE, 1.0).astype(F32)
        cos = cos_ref[...] * s
        sin = sin_ref[...] * s
        for c in range(n_chunks):
            sl = slice(c * D_HEAD, (c + 1) * D_HEAD)
            x = acc[:, sl]
            ms = jnp.mean(x * x, axis=-1, keepdims=True)
            xn = x * lax.rsqrt(ms + NORM_EPS) * g
            o_ref[:, sl] = _rope(xn, cos, sin).astype(BF16)


def _inproj_grid(m, rows_per_seq):
    bm = _pick(rows_per_seq, 1024, 8)
    return bm, (m // bm, D_IN // IN_TILE)


def _inproj(h, w, cos, sin_signed, qk_g, rows_per_seq, casts=()):
    m, d = h.shape
    bm, grid = _inproj_grid(m, rows_per_seq)
    nseq_blk = rows_per_seq // bm
    tab = pl.BlockSpec((bm, D_HEAD), lambda i, j: (i % nseq_blk, 0))
    cast_specs = [cast.specs(lambda i, j: i * grid[1] + j) for cast in casts]
    return pl.pallas_call(
        functools.partial(_inproj_kernel, casts=tuple(casts), n_col_tiles=grid[1]),
        grid=grid,
        in_specs=[pl.BlockSpec((bm, d), lambda i, j: (i, 0)),
                  pl.BlockSpec((1, d, IN_TILE), lambda i, j: (0, 0, j)),
                  tab, tab,
                  pl.BlockSpec((2, D_HEAD), lambda i, j: (0, 0)),
                  *[sp[0] for sp in cast_specs]],
        out_specs=[pl.BlockSpec((bm, IN_TILE), lambda i, j: (i, j)), *[sp[1] for sp in cast_specs]],
        out_shape=[jax.ShapeDtypeStruct((m, D_IN), BF16), *[cast.out_shape for cast in casts]],
        compiler_params=_params(("arbitrary", "arbitrary")),
        name="inproj",
    )(h, w, cos, sin_signed, qk_g, *[cast.w for cast in casts])


def _scores(q, k_refs, lo):
    return [_dot_nt(q, k_ref[:, lo:lo + D_HEAD]) for k_ref in k_refs]


def _softmax_parts(scores):
    mx = functools.reduce(jnp.maximum, [jnp.max(s, axis=-1, keepdims=True) for s in scores])
    es = [jnp.exp2(s - mx) for s in scores]
    den = functools.reduce(jnp.add, [jnp.sum(e, axis=-1, keepdims=True) for e in es])
    return es, den


def _diff_attn_kernel(*refs, n_src, lam_init):
    q_ref = refs[0]
    k_refs = refs[1:1 + n_src]
    v_refs = refs[1 + n_src:1 + 2 * n_src]
    lam_ref, g_ref, o_ref = refs[1 + 2 * n_src:]
    lp = lam_ref[...]
    lam = (jnp.exp(jnp.sum(lp[0:1] * lp[1:2], axis=-1, keepdims=True))
           - jnp.exp(jnp.sum(lp[2:3] * lp[3:4], axis=-1, keepdims=True)) + lam_init)
    rows = min(ATTN_ROWS, q_ref.shape[0])

    def issue(c):
        q = q_ref[c * rows:(c + 1) * rows, :]
        return _scores(q[:, :D_HEAD], k_refs, 0), _scores(q[:, D_HEAD:], k_refs, D_HEAD)

    def finish(c, scores):
        e1, den1 = _softmax_parts(scores[0])
        e2, den2 = _softmax_parts(scores[1])
        a1 = 1.0 / den1
        a2 = lam / den2
        o = None
        for x1, x2, v_ref in zip(e1, e2, v_refs):
            p = (x1 * a1 - x2 * a2).astype(BF16)
            pv = _dot(p, v_ref[...])
            o = pv if o is None else o + pv
        ms = jnp.mean(o * o, axis=-1, keepdims=True)
        y = (o * lax.rsqrt(ms + NORM_EPS)) * g_ref[...] * (1.0 - lam_init)
        o_ref[c * rows:(c + 1) * rows, :] = y.astype(BF16)

    _pipelined(q_ref.shape[0] // rows, issue, finish)


def _diff_attention(pq, kv_sources, lam_p, subln_g, lam_init, batch, q_len, bq):
    nq = q_len // bq
    kblk, vblk = OFF_AK // DIFF_W, OFF_AV // DIFF_W
    k_specs = [pl.BlockSpec((n, DIFF_W), lambda b, h, i: (b, kblk + h)) for _, n in kv_sources]
    v_specs = [pl.BlockSpec((n, DIFF_W), lambda b, h, i: (b, vblk + h)) for _, n in kv_sources]
    srcs = [p for p, _ in kv_sources]
    kernel = functools.partial(_diff_attn_kernel, n_src=len(srcs), lam_init=lam_init)
    return pl.pallas_call(
        kernel,
        grid=(batch, H_DIFF, nq),
        in_specs=[pl.BlockSpec((bq, DIFF_W), lambda b, h, i: (b * nq + i, h)),
                  *k_specs, *v_specs,
                  pl.BlockSpec((4, D_HEAD), lambda b, h, i: (0, 0)),
                  pl.BlockSpec((1, DIFF_W), lambda b, h, i: (0, 0))],
        out_specs=pl.BlockSpec((bq, DIFF_W), lambda b, h, i: (b * nq + i, h)),
        out_shape=jax.ShapeDtypeStruct((batch * q_len, H_DIFF * DIFF_W), BF16),
        compiler_params=_params(("arbitrary", "arbitrary", "arbitrary")),
        name="diff_attention",
    )(pq, *srcs, *srcs, lam_p, subln_g.reshape(1, DIFF_W))


def _gqa_kernel(*refs, n_src):
    q_ref = refs[0]
    k_refs = refs[1:1 + n_src]
    v_refs = refs[1 + n_src:1 + 2 * n_src]
    o_ref = refs[1 + 2 * n_src]
    rows = min(ATTN_ROWS // 2, q_ref.shape[0])
    heads = [slice(g * D_HEAD, (g + 1) * D_HEAD) for g in range(GQA_GROUP)]

    def issue(c):
        rs = slice(c * rows, (c + 1) * rows)
        q = jnp.concatenate([q_ref[rs, cs] for cs in heads], axis=0)
        return _scores(q, k_refs, 0)

    def finish(c, scores):
        rs = slice(c * rows, (c + 1) * rows)
        es, den = _softmax_parts(scores)
        o = None
        for e, v_ref in zip(es, v_refs):
            pv = _dot(e.astype(BF16), v_ref[...])
            o = pv if o is None else o + pv
        o = (o / den).astype(BF16)
        for g, cs in enumerate(heads):
            o_ref[rs, cs] = o[g * rows:(g + 1) * rows]

    _pipelined(q_ref.shape[0] // rows, issue, finish)


def _gqa_attention(pq, kv_sources, batch, q_len, bq):
    nq = q_len // bq
    qblk, kblk, vblk = OFF_CQ // GQA_W, OFF_CK // D_HEAD, OFF_CV // D_HEAD
    k_specs = [pl.BlockSpec((n, D_HEAD), lambda b, kh, i: (b, kblk + kh)) for _, n in kv_sources]
    v_specs = [pl.BlockSpec((n, D_HEAD), lambda b, kh, i: (b, vblk + kh)) for _, n in kv_sources]
    srcs = [p for p, _ in kv_sources]
    return pl.pallas_call(
        functools.partial(_gqa_kernel, n_src=len(srcs)),
        grid=(batch, H_KV, nq),
        in_specs=[pl.BlockSpec((bq, GQA_W), lambda b, kh, i: (b * nq + i, qblk + kh)),
                  *k_specs, *v_specs],
        out_specs=pl.BlockSpec((bq, GQA_W), lambda b, kh, i: (b * nq + i, kh)),
        out_shape=jax.ShapeDtypeStruct((batch * q_len, H_GQA * D_HEAD), BF16),
        compiler_params=_params(("arbitrary",) * 3),
        name="gqa_attention",
    )(pq, *srcs, *srcs)


def _dft_tables(n, split=64):
    k = jnp.arange(n, dtype=jnp.int32)

    def cos_sin(t):
        ang = ((k[:, None] * t[None, :]) % n).astype(F32) * (2.0 * math.pi / n)
        return jnp.cos(ang), jnp.sin(ang)

    if n <= split or n % split:
        c, s = cos_sin(k)
    else:
        ca, sa = (v[:, :, None] for v in cos_sin(jnp.arange(0, n, split, dtype=jnp.int32)))
        cb, sb = (v[:, None, :] for v in cos_sin(jnp.arange(split, dtype=jnp.int32)))
        c = (ca * cb - sa * sb).reshape(n, n)
        s = (sa * cb + ca * sb).reshape(n, n)
    return c.astype(BF16), s.astype(BF16)


def _fnet_kernel(cn_ref, sn_ref, u_ref, cc_ref, sc_ref, o_ref, *, scale):
    u = u_ref[...]
    t1 = _dot(cn_ref[...], u).astype(BF16)
    t2 = _dot(sn_ref[...], u).astype(BF16)
    for g in range(u.shape[1] // FNET_GROUP_W):
        sl = slice(g * FNET_GROUP_W, (g + 1) * FNET_GROUP_W)
        y = _dot(t1[:, sl], cc_ref[...]) - _dot(t2[:, sl], sc_ref[...])
        o_ref[:, sl] = (y * scale).astype(BF16)


def _fourier_mix(p, batch, n, pos_tabs, chan_tabs):
    bm = _pick(n, 1024, 8)
    bc = 2 * FNET_GROUP_W
    ni = n // bm
    ublk = OFF_B // bc
    pos = pl.BlockSpec((bm, n), lambda b, i, j: (i, 0))
    chan = pl.BlockSpec((FNET_GROUP_W, FNET_GROUP_W), lambda b, i, j: (0, 0))
    scale = 1.0 / math.sqrt(n * FNET_GROUP_W)
    return pl.pallas_call(
        functools.partial(_fnet_kernel, scale=scale),
        grid=(batch, ni, FNET_GROUPS * FNET_GROUP_W // bc),
        in_specs=[pos, pos, pl.BlockSpec((n, bc), lambda b, i, j: (b, ublk + j)), chan, chan],
        out_specs=pl.BlockSpec((bm, bc), lambda b, i, j: (b * ni + i, j)),
        out_shape=jax.ShapeDtypeStruct((batch * n, FNET_GROUPS * FNET_GROUP_W), BF16),
        compiler_params=_params(("arbitrary",) * 3),
        name="fourier_mix",
    )(*pos_tabs, p, *chan_tabs)


def _merge_kernel(h_ref, oa_ref, ob_ref, oc_ref, wg0_ref, wg1_ref, wg2_ref, bg0_ref, bg1_ref, bg2_ref,
                  wb_ref, o_ref):
    h = h_ref[...]
    acc = None
    for n, (br_ref, wg_ref, bg_ref) in enumerate(((oa_ref, wg0_ref, bg0_ref), (ob_ref, wg1_ref, bg1_ref),
                                                  (oc_ref, wg2_ref, bg2_ref))):
        gate = jax.nn.sigmoid(_dot(h, wg_ref[0]) + bg_ref[0])
        term = gate * _dot(br_ref[...], wb_ref[n])
        acc = term if acc is None else acc + term
    o_ref[...] = acc.astype(BF16)


def _gated_merge(h, oa, ob, oc, w_gate, b_gate, w_branch, layer):
    m, d = h.shape
    bm = _pick(m, 1024, 8)
    bn = _pick(d, 256, LANES)
    nj = d // bn
    once = pl.Buffered(1)
    br = pl.BlockSpec((bm, BRANCH_W), lambda i, j: (i, 0), pipeline_mode=once)
    wg = [pl.BlockSpec((1, d, bn), lambda i, j, n=n: (0, 0, n * nj + j)) for n in range(N_BRANCH)]
    bg = [pl.BlockSpec((1, 1, bn), lambda i, j, n=n: (layer, 0, n * nj + j)) for n in range(N_BRANCH)]
    return pl.pallas_call(
        _merge_kernel,
        grid=(m // bm, nj),
        in_specs=[pl.BlockSpec((bm, d), lambda i, j: (i, 0), pipeline_mode=once), br, br, br, *wg, *bg,
                  pl.BlockSpec((N_BRANCH, BRANCH_W, bn), lambda i, j: (0, 0, j))],
        out_specs=pl.BlockSpec((bm, bn), lambda i, j: (i, j)),
        out_shape=jax.ShapeDtypeStruct((m, d), BF16),
        compiler_params=_params(("arbitrary",) * 2),
        name="gated_merge",
    )(h, oa, ob, oc, w_gate, w_gate, w_gate, b_gate, b_gate, b_gate, w_branch)


def _conv_act_kernel(a_ref, g_ref, w_ref, b_ref, o_ref):
    a = a_ref[0].astype(F32)
    s = a.shape[0]
    row = lax.broadcasted_iota(jnp.int32, a.shape, 0)
    prev = jnp.where(row == 0, 0.0, pltpu.roll(a, 1, 0))
    nxt = jnp.where(row == s - 1, 0.0, pltpu.roll(a, s - 1, 0))
    w = w_ref[...]
    z = prev * w[0:1] + a * w[1:2] + nxt * w[2:3] + b_ref[...]
    gelu = 0.5 * z * (1.0 + lax.erf(z * math.sqrt(0.5)))
    o_ref[0] = (gelu * g_ref[0].astype(F32)).astype(BF16)


def _conv_act(u, conv_w, conv_b, batch, s):
    ffp = conv_w.shape[1]
    bc = _pick(ffp, 256, LANES)
    nj = ffp // bc
    u3 = u.reshape(batch, s, 2 * ffp)
    out = pl.pallas_call(
        _conv_act_kernel,
        grid=(batch, nj),
        in_specs=[pl.BlockSpec((1, s, bc), lambda b, j: (b, 0, j)),
                  pl.BlockSpec((1, s, bc), lambda b, j: (b, 0, nj + j)),
                  pl.BlockSpec((3, bc), lambda b, j: (0, j)),
                  pl.BlockSpec((1, bc), lambda b, j: (0, j))],
        out_specs=pl.BlockSpec((1, s, bc), lambda b, j: (b, 0, j)),
        out_shape=jax.ShapeDtypeStruct((batch, s, ffp), BF16),
        compiler_params=_params(("arbitrary", "arbitrary")),
        name="conv_act",
    )(u3, u3, conv_w, conv_b)
    return out.reshape(batch * s, ffp)


def _rope_tables(n):
    n_rows = n // GRID_W
    rows = jnp.broadcast_to(jnp.arange(n_rows)[:, None], (n_rows, GRID_W)).reshape(-1).astype(F32)
    cols = jnp.broadcast_to(jnp.arange(GRID_W)[None, :], (n_rows, GRID_W)).reshape(-1).astype(F32)
    quarter = D_HEAD // 4
    inv_freq = ROPE_THETA ** (-jnp.arange(quarter, dtype=F32) / quarter)
    ang = jnp.concatenate([rows[:, None] * inv_freq, cols[:, None] * inv_freq], axis=-1)
    ang = jnp.concatenate([ang, ang], axis=-1)
    sign = jnp.where(jnp.arange(D_HEAD) < D_HEAD // 2, -1.0, 1.0).astype(F32)
    return jnp.cos(ang), jnp.sin(ang) * sign


def kernel(x, c, ctx, c_ctx, w_ada, b_ada, w_in, diff_lambda, diff_subln_g, qk_norm_g, w_branch, w_gate,
           b_gate, w_o, ln1_g, ln1_b, w_up, conv_w, conv_b, w_down, ln2_g, ln2_b):
    batch, seq, d = x.shape
    n_ctx = ctx.shape[1]
    depth = w_ada.shape[0]
    ff = conv_w.shape[2]
    assert ff % LANES == 0 and w_in.shape[2] == D_IN
    ffp = -(-ff // 1024) * 1024 if ff > 1024 else ff
    alpha = (2 * depth) ** 0.25

    cvec = jnp.zeros((8, d), F32).at[:batch].set(c).at[batch].set(c_ctx)
    mod = _ada(cvec, w_ada, b_ada).reshape(depth, 8, 6, d)

    pad = ffp - ff
    conv_w_p = jnp.pad(conv_w, ((0, 0), (0, 0), (0, pad)))
    conv_b_p = jnp.pad(conv_b, ((0, 0), (0, pad))).reshape(depth, 1, ffp)
    b_gate3 = b_gate.reshape(depth, 1, N_BRANCH * d)
    w_branch2 = w_branch.reshape(depth, N_BRANCH * BRANCH_W, d)

    _, inproj_grid = _inproj_grid(batch * seq, seq)
    n_steps = inproj_grid[0] * inproj_grid[1]

    def layer_casts(l):
        casts = [_Cast(w_gate, l, n_steps), _Cast(w_branch2, l, n_steps), _Cast(w_o, l, n_steps),
                 _Cast(w_up, l, n_steps, nseg=2, cp=ffp), _Cast(w_down, l, n_steps, rp=ffp)]
        if l + 1 < depth:
            casts.append(_Cast(w_in, l + 1, n_steps))
        return casts

    w_in_b = _cast_now(_Cast(w_in, 0, 32))

    cos_l, sin_l = _rope_tables(seq)
    cos_c, sin_c = jnp.ones((n_ctx, D_HEAD), F32), jnp.zeros((n_ctx, D_HEAD), F32)
    pos_l, pos_c = _dft_tables(seq), _dft_tables(n_ctx)
    chan = _dft_tables(FNET_GROUP_W)

    xl = x.reshape(batch * seq, d)
    xc = ctx.reshape(batch * n_ctx, d)
    m_ctx = batch * n_ctx
    bq_diff_l = _pick(seq, 4 * ATTN_ROWS, 8)
    bq_gqa_l = _pick(seq, 2 * ATTN_ROWS, 8)
    bq_c = _pick(n_ctx, ATTN_ROWS, 8)
    bk_ff = _pick(ffp, 2816, LANES)

    def lat(i, l):
        return mod[l, :batch, i][:, None, :]

    def cx(i, l):
        return mod[l, batch, i][None, None, :]

    hl = _modulate(xl, lat(1, 0), lat(0, 0), seq)
    hc = _modulate(xc, cx(1, 0), cx(0, 0), m_ctx)

    for l in range(depth):
        last = l == depth - 1
        lam_init = 0.8 - 0.6 * math.exp(-0.3 * l)

        pl_, w_gate_b, w_branch_b, w_o_b, w_up_b, w_down_b, *w_in_next = _inproj(
            hl, w_in_b, cos_l, sin_l, qk_norm_g[l], seq, casts=layer_casts(l))
        w_branch_b = w_branch_b.reshape(N_BRANCH, BRANCH_W, d)
        (pc_,) = _inproj(hc, w_in_b, cos_c, sin_c, qk_norm_g[l], n_ctx)
        kv = [(pl_, seq), (pc_, n_ctx)]
        oa = _diff_attention(pl_, kv, diff_lambda[l], diff_subln_g[l], lam_init, batch, seq, bq_diff_l)
        ob = _fourier_mix(pl_, batch, seq, pos_l, chan)
        oc = _gqa_attention(pl_, kv, batch, seq, bq_gqa_l)
        merged = _gated_merge(hl, oa, ob, oc, w_gate_b, b_gate3, w_branch_b, l)
        yl = _matmul_residual(merged, w_o_b, xl, lat(2, l), seq, alpha, bk=d, name="out_proj")
        xl, hl = _ln(yl, ln1_g[l], ln1_b[l], mod=(lat(4, l), lat(3, l)), rows_per_group=seq)

        if not last:
            kvc = [(pc_, n_ctx)]
            oa_c = _diff_attention(pc_, kvc, diff_lambda[l], diff_subln_g[l], lam_init, batch, n_ctx, bq_c)
            ob_c = _fourier_mix(pc_, batch, n_ctx, pos_c, chan)
            oc_c = _gqa_attention(pc_, kvc, batch, n_ctx, bq_c)
            merged_c = _gated_merge(hc, oa_c, ob_c, oc_c, w_gate_b, b_gate3, w_branch_b, l)
            yc = _matmul_residual(merged_c, w_o_b, xc, cx(2, l), m_ctx, alpha, bk=d, name="out_proj_ctx")
            xc, hc = _ln(yc, ln1_g[l], ln1_b[l], mod=(cx(4, l), cx(3, l)), rows_per_group=m_ctx)

        def ffn(h, xres, gate, rows_per_group, s):
            mrows = h.shape[0]
            u = _matmul(h, w_up_b, bm=_pick(rows_per_group, 1024, 8), bn=_pick(2 * ffp, 1024, LANES), bk=d,
                        out_dtype=BF16, epilogue=_identity, name="ffn_up")
            act = _conv_act(u, conv_w_p[l], conv_b_p[l], mrows // s, s)
            return _matmul_residual(act, w_down_b, xres, gate, rows_per_group, alpha, bk=bk_ff,
                                    name="ffn_down")

        yl = ffn(hl, xl, lat(5, l), seq, seq)
        if last:
            xl = _ln(yl, ln2_g[l], ln2_b[l])
        else:
            xl, hl = _ln(yl, ln2_g[l], ln2_b[l], mod=(lat(1, l + 1), lat(0, l + 1)), rows_per_group=seq)
            yc = ffn(hc, xc, cx(5, l), m_ctx, n_ctx)
            xc, hc = _ln(yc, ln2_g[l], ln2_b[l], mod=(cx(1, l + 1), cx(0, l + 1)), rows_per_group=m_ctx)

        if not last:
            (w_in_b,) = w_in_next

    return xl.reshape(batch, seq, d)
```

```python
import functools
import math

import jax
import jax.numpy as jnp
from jax import lax
from jax.experimental import pallas as pl
from jax.experimental.pallas import tpu as pltpu

F32 = jnp.float32
BF16 = jnp.bfloat16

GRID_W = 64
D_HEAD = 128
H_DIFF = 8
DIFF_W = 2 * D_HEAD
FNET_GROUPS = 8
FNET_GROUP_W = 256
H_GQA = 16
H_KV = 4
GQA_GROUP = H_GQA // H_KV
GQA_W = GQA_GROUP * D_HEAD
BRANCH_W = 2048
N_BRANCH = 3
ROPE_THETA = 10000.0
NORM_EPS = 1e-6

OFF_AQ = 0
OFF_AK = OFF_AQ + H_DIFF * DIFF_W
OFF_AV = OFF_AK + H_DIFF * DIFF_W
OFF_B = OFF_AV + H_DIFF * DIFF_W
OFF_CQ = OFF_B + FNET_GROUPS * FNET_GROUP_W
OFF_CK = OFF_CQ + H_GQA * D_HEAD
OFF_CV = OFF_CK + H_KV * D_HEAD
D_IN = OFF_CV + H_KV * D_HEAD
Q_SCALE = D_HEAD ** -0.5 * math.log2(math.e)

V7X_VMEM_LIMIT_BYTES = 56 * 1024 * 1024
LANES = 128
IN_TILE = 512
ATTN_ROWS = 256


def _params(sem):
    return pltpu.CompilerParams(dimension_semantics=sem, vmem_limit_bytes=V7X_VMEM_LIMIT_BYTES)


def _pick(dim, pref, align):
    t = (min(pref, dim) // align) * align
    while t >= align:
        if dim % t == 0:
            return t
        t -= align
    return dim


def _dot(a, b):
    return jnp.dot(a, b, preferred_element_type=F32)


def _dot_nt(a, b):
    return lax.dot_general(a, b, (((1,), (1,)), ((), ())), preferred_element_type=F32)


def _pipelined(n_chains, issue, finish, lookahead=1):
    pending = [issue(c) for c in range(min(lookahead, n_chains))]
    for c in range(n_chains):
        if c + lookahead < n_chains:
            pending.append(issue(c + lookahead))
        finish(c, pending.pop(0))


class _Cast:
    def __init__(self, w, layer, max_blocks, *, nseg=1, cp=None, rp=None):
        _, r, c_all = w.shape
        self.w, self.layer, self.nseg = w, layer, nseg
        self.c = c_all // nseg
        self.cp = cp or self.c
        rows_out = rp or r
        div = math.gcd(r, rows_out)
        self.rb = next(rb for rb in range(16, div + 1, 16) if div % rb == 0 and rows_out // rb <= max_blocks)
        self.n_in, self.n_out = r // self.rb, rows_out // self.rb
        self.out_shape = jax.ShapeDtypeStruct((1, rows_out, nseg * self.cp), BF16)

    def specs(self, step):
        out_blk = lambda *g: jnp.minimum(step(*g), self.n_out - 1)
        return (pl.BlockSpec((1, self.rb, self.nseg * self.c),
                             lambda *g: (self.layer, jnp.minimum(out_blk(*g), self.n_in - 1), 0)),
                pl.BlockSpec((1, self.rb, self.nseg * self.cp), lambda *g: (0, out_blk(*g), 0)))

    def body(self, t, x_ref, o_ref):
        x = x_ref[0]
        if self.n_out > self.n_in:
            x = jnp.where(jnp.minimum(t, self.n_out - 1) < self.n_in, x, 0.0)
        for s in range(self.nseg):
            o_ref[0, :, s * self.cp:s * self.cp + self.c] = x[:, s * self.c:(s + 1) * self.c].astype(BF16)
            if self.cp > self.c:
                o_ref[0, :, s * self.cp + self.c:(s + 1) * self.cp] = jnp.zeros((self.rb, self.cp - self.c), BF16)


def _run_casts(t, casts, refs):
    for n, cast in enumerate(casts):
        cast.body(t, refs[n], refs[len(casts) + n])


def _cast_kernel(x_ref, o_ref, *, cast):
    cast.body(pl.program_id(0), x_ref, o_ref)


def _cast_now(cast):
    in_spec, out_spec = cast.specs(lambda t: t)
    return pl.pallas_call(
        functools.partial(_cast_kernel, cast=cast), grid=(cast.n_out,), in_specs=[in_spec],
        out_specs=out_spec, out_shape=cast.out_shape,
        compiler_params=_params(("arbitrary",)), name="cast_weight",
    )(cast.w)


def _bf16_pieces(x, n):
    pieces = []
    for _ in range(n):
        p = x.astype(BF16).astype(F32)
        pieces.append(p)
        x = x - p
    return pieces


def _ada_kernel(c_ref, w_ref, b_ref, o_ref):
    c = c_ref[...]
    rows = c.shape[0]
    s3 = jnp.concatenate(_bf16_pieces(c * jax.nn.sigmoid(c), 3), axis=0).astype(BF16)
    w_hi, w_lo = _bf16_pieces(w_ref[0], 2)
    hi = _dot(s3, w_hi.astype(BF16))
    lo = _dot(s3, w_lo.astype(BF16))
    acc = (hi[:rows] + hi[rows:2 * rows] + hi[2 * rows:]) + (lo[:rows] + lo[rows:2 * rows])
    o_ref[0] = acc + b_ref[0]


def _ada(cvec, w_ada, b_ada):
    depth, d, n = w_ada.shape
    bn = _pick(n, 512, LANES)
    return pl.pallas_call(
        _ada_kernel,
        grid=(depth, n // bn),
        in_specs=[pl.BlockSpec((8, d), lambda l, j: (0, 0)),
                  pl.BlockSpec((1, d, bn), lambda l, j: (l, 0, j)),
                  pl.BlockSpec((1, 1, bn), lambda l, j: (l, 0, j))],
        out_specs=pl.BlockSpec((1, 8, bn), lambda l, j: (l, 0, j)),
        out_shape=jax.ShapeDtypeStruct((depth, 8, n), F32),
        compiler_params=_params(("arbitrary", "arbitrary")),
        name="ada",
    )(cvec, w_ada, b_ada.reshape(depth, 1, n))


def _mod_kernel(x_ref, sc_ref, sh_ref, h_ref):
    h_ref[...] = (x_ref[...] * (1.0 + sc_ref[0]) + sh_ref[0]).astype(BF16)


def _modulate(x, sc, sh, rows_per_group):
    m, d = x.shape
    bs = _pick(rows_per_group, 256, 8)
    grp = lambda i: (i * bs // rows_per_group, 0, 0)
    return pl.pallas_call(
        _mod_kernel,
        grid=(m // bs,),
        in_specs=[pl.BlockSpec((bs, d), lambda i: (i, 0)),
                  pl.BlockSpec((1, 1, d), grp),
                  pl.BlockSpec((1, 1, d), grp)],
        out_specs=pl.BlockSpec((bs, d), lambda i: (i, 0)),
        out_shape=jax.ShapeDtypeStruct((m, d), BF16),
        compiler_params=_params(("arbitrary",)),
        name="modulate",
    )(x, sc, sh)


def _layer_norm(y, g, b):
    mu = jnp.mean(y, axis=-1, keepdims=True)
    dlt = y - mu
    var = jnp.mean(dlt * dlt, axis=-1, keepdims=True)
    return dlt * lax.rsqrt(var + NORM_EPS) * g + b


def _ln_mod_kernel(y_ref, g_ref, b_ref, sc_ref, sh_ref, x_ref, h_ref):
    xn = _layer_norm(y_ref[...], g_ref[...], b_ref[...])
    x_ref[...] = xn
    h_ref[...] = (xn * (1.0 + sc_ref[0]) + sh_ref[0]).astype(BF16)


def _ln_kernel(y_ref, g_ref, b_ref, x_ref):
    x_ref[...] = _layer_norm(y_ref[...], g_ref[...], b_ref[...])


def _ln(y, g, b, mod=None, rows_per_group=None):
    m, d = y.shape
    bs = _pick(rows_per_group or m, 256, 8)
    row = pl.BlockSpec((bs, d), lambda i: (i, 0))
    vec = pl.BlockSpec((1, d), lambda i: (0, 0))
    g2, b2 = g.reshape(1, d), b.reshape(1, d)
    if mod is None:
        return pl.pallas_call(
            _ln_kernel, grid=(m // bs,), in_specs=[row, vec, vec], out_specs=row,
            out_shape=jax.ShapeDtypeStruct((m, d), F32),
            compiler_params=_params(("arbitrary",)), name="layernorm",
        )(y, g2, b2)
    grp = pl.BlockSpec((1, 1, d), lambda i: (i * bs // rows_per_group, 0, 0))
    return pl.pallas_call(
        _ln_mod_kernel, grid=(m // bs,), in_specs=[row, vec, vec, grp, grp],
        out_specs=[row, row],
        out_shape=[jax.ShapeDtypeStruct((m, d), F32), jax.ShapeDtypeStruct((m, d), BF16)],
        compiler_params=_params(("arbitrary",)), name="layernorm_mod",
    )(y, g2, b2, mod[0], mod[1])


def _mm_kernel(*refs, nk, n_extra, epilogue):
    x_ref, w_ref = refs[0], refs[1]
    extra = refs[2:2 + n_extra]
    o_ref = refs[2 + n_extra]
    part = _dot(x_ref[...], w_ref[0])
    if nk == 1:
        o_ref[...] = epilogue(part, *extra).astype(o_ref.dtype)
        return
    acc_ref = refs[3 + n_extra]
    k = pl.program_id(2)

    @pl.when(k == 0)
    def _():
        acc_ref[...] = part

    @pl.when((k > 0) & (k < nk - 1))
    def _():
        acc_ref[...] += part

    @pl.when(k == nk - 1)
    def _():
        o_ref[...] = epilogue(acc_ref[...] + part, *extra).astype(o_ref.dtype)


def _matmul(x, w, *, bm, bn, bk, out_dtype, epilogue, extra=(), extra_specs=(), name):
    m, kdim = x.shape
    n = w.shape[2]
    nk = kdim // bk
    kernel = functools.partial(_mm_kernel, nk=nk, n_extra=len(extra), epilogue=epilogue)
    return pl.pallas_call(
        kernel,
        grid=(m // bm, n // bn, nk),
        in_specs=[pl.BlockSpec((bm, bk), lambda i, j, k: (i, k)),
                  pl.BlockSpec((1, bk, bn), lambda i, j, k: (0, k, j)),
                  *extra_specs],
        out_specs=pl.BlockSpec((bm, bn), lambda i, j, k: (i, j)),
        out_shape=jax.ShapeDtypeStruct((m, n), out_dtype),
        scratch_shapes=[pltpu.VMEM((bm, bn), F32)] if nk > 1 else [],
        compiler_params=_params(("arbitrary", "arbitrary", "arbitrary")),
        name=name,
    )(x, w, *extra)


def _identity(acc):
    return acc


def _residual_epilogue(acc, x_ref, gate_ref, *, alpha):
    return alpha * x_ref[...] + gate_ref[0] * acc


def _matmul_residual(a, w, xres, gate, rows_per_group, alpha, *, bk, name):
    n = w.shape[2]
    bm = _pick(rows_per_group, 1024, 8)
    bn = _pick(n, 1024, LANES)
    return _matmul(
        a, w, bm=bm, bn=bn, bk=bk, out_dtype=F32,
        epilogue=functools.partial(_residual_epilogue, alpha=alpha),
        extra=(xres, gate),
        extra_specs=(pl.BlockSpec((bm, bn), lambda i, j, k: (i, j)),
                     pl.BlockSpec((1, 1, bn), lambda i, j, k: (i * bm // rows_per_group, 0, j))),
        name=name)


def _rope(x, cos, sin_signed):
    return x * cos + pltpu.roll(x, D_HEAD // 2, 1) * sin_signed


def _inproj_kernel(h_ref, w_ref, cos_ref, sin_ref, g_ref, *refs, casts, n_col_tiles):
    o_ref = refs[len(casts)]
    i, j = pl.program_id(0), pl.program_id(1)
    _run_casts(i * n_col_tiles + j, casts, refs[:len(casts)] + refs[len(casts) + 1:])
    acc = _dot(h_ref[...], w_ref[0])
    n_chunks = IN_TILE // D_HEAD
    t_ak, t_av, t_cq, t_ck, t_cv = (OFF_AK // IN_TILE, OFF_AV // IN_TILE, OFF_CQ // IN_TILE,
                                    OFF_CK // IN_TILE, OFF_CV // IN_TILE)

    @pl.when(j < t_av)
    def _():
        s = jnp.where(j < t_ak, Q_SCALE, 1.0).astype(F32)
        cos = cos_ref[...] * s
        sin = sin_ref[...] * s
        for c in range(n_chunks):
            sl = slice(c * D_HEAD, (c + 1) * D_HEAD)
            o_ref[:, sl] = _rope(acc[:, sl], cos, sin).astype(BF16)

    @pl.when(((j >= t_av) & (j < t_cq)) | (j >= t_cv))
    def _():
        o_ref[...] = acc.astype(BF16)

    @pl.when((j >= t_cq) & (j < t_cv))
    def _():
        is_q = j < t_ck
        g = jnp.where(is_q, g_ref[0:1, :], g_ref[1:2, :])
        s = jnp.where(is_q, Q_SCALE, 1.0).astype(F32)
        cos = cos_ref[...] * s
        sin = sin_ref[...] * s
        for c in range(n_chunks):
            sl = slice(c * D_HEAD, (c + 1) * D_HEAD)
            x = acc[:, sl]
            ms = jnp.mean(x * x, axis=-1, keepdims=True)
            xn = x * lax.rsqrt(ms + NORM_EPS) * g
            o_ref[:, sl] = _rope(xn, cos, sin).astype(BF16)


def _inproj_grid(m, rows_per_seq):
    bm = _pick(rows_per_seq, 1024, 8)
    return bm, (m // bm, D_IN // IN_TILE)


def _inproj(h, w, cos, sin_signed, qk_g, rows_per_seq, casts=()):
    m, d = h.shape
    bm, grid = _inproj_grid(m, rows_per_seq)
    nseq_blk = rows_per_seq // bm
    tab = pl.BlockSpec((bm, D_HEAD), lambda i, j: (i % nseq_blk, 0))
    cast_specs = [cast.specs(lambda i, j: i * grid[1] + j) for cast in casts]
    return pl.pallas_call(
        functools.partial(_inproj_kernel, casts=tuple(casts), n_col_tiles=grid[1]),
        grid=grid,
        in_specs=[pl.BlockSpec((bm, d), lambda i, j: (i, 0)),
                  pl.BlockSpec((1, d, IN_TILE), lambda i, j: (0, 0, j)),
                  tab, tab,
                  pl.BlockSpec((2, D_HEAD), lambda i, j: (0, 0)),
                  *[sp[0] for sp in cast_specs]],
        out_specs=[pl.BlockSpec((bm, IN_TILE), lambda i, j: (i, j)), *[sp[1] for sp in cast_specs]],
        out_shape=[jax.ShapeDtypeStruct((m, D_IN), BF16), *[cast.out_shape for cast in casts]],
        compiler_params=_params(("arbitrary", "arbitrary")),
        name="inproj",
    )(h, w, cos, sin_signed, qk_g, *[cast.w for cast in casts])


KEY_CHUNK = 1024


class _SoftmaxChain:
    def __init__(self, q, k_refs, v_refs, k_lo, v_lo, v_width):
        self.q, self.k_refs, self.v_refs = q, k_refs, v_refs
        self.k_lo, self.v_lo, self.v_width = k_lo, v_lo, v_width
        self.chunks = [(n, c0, min(c0 + KEY_CHUNK, k_ref.shape[0]))
                       for n, k_ref in enumerate(k_refs) for c0 in range(0, k_ref.shape[0], KEY_CHUNK)]
        self.scores = []
        self.lane_max = self.lane_sum = self.acc = self.row_max = None

    def score(self, i):
        n, c0, c1 = self.chunks[i]
        s = _dot_nt(self.q, self.k_refs[n][c0:c1, self.k_lo:self.k_lo + D_HEAD])
        self.scores.append(s)
        for t in range(0, c1 - c0, LANES):
            tile = s[:, t:t + LANES]
            self.lane_max = tile if self.lane_max is None else jnp.maximum(self.lane_max, tile)

    def accumulate(self, i):
        if self.row_max is None:
            self.row_max = jnp.max(self.lane_max, axis=-1, keepdims=True)
        n, c0, c1 = self.chunks[i]
        e = jnp.exp2(self.scores[i] - self.row_max)
        for t in range(0, c1 - c0, LANES):
            tile = e[:, t:t + LANES]
            self.lane_sum = tile if self.lane_sum is None else self.lane_sum + tile
        pv = _dot(e.astype(BF16), self.v_refs[n][c0:c1, self.v_lo:self.v_lo + self.v_width])
        self.acc = pv if self.acc is None else self.acc + pv

    def result(self):
        return self.acc, jnp.sum(self.lane_sum, axis=-1, keepdims=True)


def _run_chains(n_blocks, make_chains, write):
    cur = make_chains(0)
    for i in range(len(cur[0].chunks)):
        for ch in cur:
            ch.score(i)
    for b in range(n_blocks):
        nxt = make_chains(b + 1) if b + 1 < n_blocks else []
        for i in range(len(cur[0].chunks)):
            for ch in nxt:
                ch.score(i)
            for ch in cur:
                ch.accumulate(i)
        write(b, cur)
        cur = nxt


def _diff_attn_kernel(*refs, n_src, lam_init):
    q_ref = refs[0]
    k_refs = refs[1:1 + n_src]
    v_refs = refs[1 + n_src:1 + 2 * n_src]
    lam_ref, g_ref, o_ref = refs[1 + 2 * n_src:]
    lp = lam_ref[...]
    lam = (jnp.exp(jnp.sum(lp[0:1] * lp[1:2], axis=-1, keepdims=True))
           - jnp.exp(jnp.sum(lp[2:3] * lp[3:4], axis=-1, keepdims=True)) + lam_init)
    rows = min(ATTN_ROWS, q_ref.shape[0])

    def make_chains(b):
        q = q_ref[b * rows:(b + 1) * rows, :]
        return [_SoftmaxChain(q[:, lo:lo + D_HEAD], k_refs, v_refs, lo, 0, DIFF_W) for lo in (0, D_HEAD)]

    def write(b, chains):
        (o1, den1), (o2, den2) = chains[0].result(), chains[1].result()
        o = o1 * (1.0 / den1) - o2 * (lam / den2)
        ms = jnp.mean(o * o, axis=-1, keepdims=True)
        y = (o * lax.rsqrt(ms + NORM_EPS)) * g_ref[...] * (1.0 - lam_init)
        o_ref[b * rows:(b + 1) * rows, :] = y.astype(BF16)

    _run_chains(q_ref.shape[0] // rows, make_chains, write)


def _diff_attention(pq, kv_sources, lam_p, subln_g, lam_init, batch, q_len, bq):
    nq = q_len // bq
    kblk, vblk = OFF_AK // DIFF_W, OFF_AV // DIFF_W
    k_specs = [pl.BlockSpec((n, DIFF_W), lambda b, h, i: (b, kblk + h)) for _, n in kv_sources]
    v_specs = [pl.BlockSpec((n, DIFF_W), lambda b, h, i: (b, vblk + h)) for _, n in kv_sources]
    srcs = [p for p, _ in kv_sources]
    kernel = functools.partial(_diff_attn_kernel, n_src=len(srcs), lam_init=lam_init)
    return pl.pallas_call(
        kernel,
        grid=(batch, H_DIFF, nq),
        in_specs=[pl.BlockSpec((bq, DIFF_W), lambda b, h, i: (b * nq + i, h)),
                  *k_specs, *v_specs,
                  pl.BlockSpec((4, D_HEAD), lambda b, h, i: (0, 0)),
                  pl.BlockSpec((1, DIFF_W), lambda b, h, i: (0, 0))],
        out_specs=pl.BlockSpec((bq, DIFF_W), lambda b, h, i: (b * nq + i, h)),
        out_shape=jax.ShapeDtypeStruct((batch * q_len, H_DIFF * DIFF_W), BF16),
        compiler_params=_params(("arbitrary", "arbitrary", "arbitrary")),
        name="diff_attention",
    )(pq, *srcs, *srcs, lam_p, subln_g.reshape(1, DIFF_W))


def _gqa_kernel(*refs, n_src):
    q_ref = refs[0]
    k_refs = refs[1:1 + n_src]
    v_refs = refs[1 + n_src:1 + 2 * n_src]
    o_ref = refs[1 + 2 * n_src]
    rows = min(ATTN_ROWS // 2, q_ref.shape[0])
    heads = [slice(g * D_HEAD, (g + 1) * D_HEAD) for g in range(GQA_GROUP)]

    def make_chains(b):
        rs = slice(b * rows, (b + 1) * rows)
        q = jnp.concatenate([q_ref[rs, cs] for cs in heads], axis=0)
        return [_SoftmaxChain(q, k_refs, v_refs, 0, 0, D_HEAD)]

    def write(b, chains):
        o, den = chains[0].result()
        o = (o / den).astype(BF16)
        for g, cs in enumerate(heads):
            o_ref[b * rows:(b + 1) * rows, cs] = o[g * rows:(g + 1) * rows]

    _run_chains(q_ref.shape[0] // rows, make_chains, write)


def _gqa_attention(pq, kv_sources, batch, q_len, bq):
    nq = q_len // bq
    qblk, kblk, vblk = OFF_CQ // GQA_W, OFF_CK // D_HEAD, OFF_CV // D_HEAD
    k_specs = [pl.BlockSpec((n, D_HEAD), lambda b, kh, i: (b, kblk + kh)) for _, n in kv_sources]
    v_specs = [pl.BlockSpec((n, D_HEAD), lambda b, kh, i: (b, vblk + kh)) for _, n in kv_sources]
    srcs = [p for p, _ in kv_sources]
    return pl.pallas_call(
        functools.partial(_gqa_kernel, n_src=len(srcs)),
        grid=(batch, H_KV, nq),
        in_specs=[pl.BlockSpec((bq, GQA_W), lambda b, kh, i: (b * nq + i, qblk + kh)),
                  *k_specs, *v_specs],
        out_specs=pl.BlockSpec((bq, GQA_W), lambda b, kh, i: (b * nq + i, kh)),
        out_shape=jax.ShapeDtypeStruct((batch * q_len, H_GQA * D_HEAD), BF16),
        compiler_params=_params(("arbitrary",) * 3),
        name="gqa_attention",
    )(pq, *srcs, *srcs)


def _dft_tables(n, split=64):
    k = jnp.arange(n, dtype=jnp.int32)

    def cos_sin(t):
        ang = ((k[:, None] * t[None, :]) % n).astype(F32) * (2.0 * math.pi / n)
        return jnp.cos(ang), jnp.sin(ang)

    if n <= split or n % split:
        c, s = cos_sin(k)
    else:
        ca, sa = (v[:, :, None] for v in cos_sin(jnp.arange(0, n, split, dtype=jnp.int32)))
        cb, sb = (v[:, None, :] for v in cos_sin(jnp.arange(split, dtype=jnp.int32)))
        c = (ca * cb - sa * sb).reshape(n, n)
        s = (sa * cb + ca * sb).reshape(n, n)
    return c.astype(BF16), s.astype(BF16)


FNET_RADIX = 4


def _fnet_twiddles(n):
    p = jnp.arange(1, FNET_RADIX, dtype=jnp.int32)[:, None]
    k = jnp.arange(n // FNET_RADIX, dtype=jnp.int32)[None, :]
    ang = ((p * k) % n).astype(F32) * (2.0 * math.pi / n)
    return jnp.cos(ang)[:, :, None], jnp.sin(ang)[:, :, None]


def _fnet_kernel(c_ref, s_ref, x0_ref, x1_ref, x2_ref, x3_ref, twc_ref, tws_ref, cc_ref, sc_ref, o_ref, *, scale):
    c, s = c_ref[...], s_ref[...]
    ys = []
    for p, x_ref in enumerate((x0_ref, x1_ref, x2_ref, x3_ref)):
        x = x_ref[...]
        zr, zi = _dot(c, x), -_dot(s, x)
        if p:
            tc, ts = twc_ref[p - 1], tws_ref[p - 1]
            zr, zi = zr * tc + zi * ts, zi * tc - zr * ts
        ys.append((zr, zi))
    (ar, ai), (br, bi), (cr, ci), (dr, di) = ys
    zs = ((ar + br + cr + dr, ai + bi + ci + di),
          (ar + bi - cr - di, ai - br - ci + dr),
          (ar - br + cr - dr, ai - bi + ci - di),
          (ar - bi - cr + di, ai + br - ci - dr))
    for q, (zr, zi) in enumerate(zs):
        zr, zi = zr.astype(BF16), zi.astype(BF16)
        for g in range(zr.shape[1] // FNET_GROUP_W):
            sl = slice(g * FNET_GROUP_W, (g + 1) * FNET_GROUP_W)
            y = _dot(zr[:, sl], cc_ref[...]) + _dot(zi[:, sl], sc_ref[...])
            o_ref[0, q, :, sl] = (y * scale).astype(BF16)


def _fourier_mix(p, batch, n, pos_tabs, twiddles, chan_tabs):
    nr = n // FNET_RADIX
    bk = _pick(nr, 512, 16)
    bc = 2 * FNET_GROUP_W
    width = FNET_GROUPS * FNET_GROUP_W
    pos = pl.BlockSpec((bk, nr), lambda b, i, j: (i, 0))
    tw = pl.BlockSpec((FNET_RADIX - 1, bk, 1), lambda b, i, j: (0, i, 0))
    chan = pl.BlockSpec((FNET_GROUP_W, FNET_GROUP_W), lambda b, i, j: (0, 0))
    p4 = p.reshape(batch * nr, FNET_RADIX * D_IN)
    xs = [pl.BlockSpec((nr, bc), lambda b, i, j, r=r: (b, (r * D_IN + OFF_B) // bc + j)) for r in range(FNET_RADIX)]
    scale = 1.0 / math.sqrt(n * FNET_GROUP_W)
    out = pl.pallas_call(
        functools.partial(_fnet_kernel, scale=scale),
        grid=(batch, nr // bk, width // bc),
        in_specs=[pos, pos, *xs, tw, tw, chan, chan],
        out_specs=pl.BlockSpec((1, FNET_RADIX, bk, bc), lambda b, i, j: (b, 0, i, j)),
        out_shape=jax.ShapeDtypeStruct((batch, FNET_RADIX, nr, width), BF16),
        compiler_params=_params(("arbitrary",) * 3),
        name="fourier_mix",
    )(*pos_tabs, p4, p4, p4, p4, *twiddles, *chan_tabs)
    return out.reshape(batch * n, width)


def _merge_kernel(h_ref, oa_ref, ob_ref, oc_ref, wg0_ref, wg1_ref, wg2_ref, bg0_ref, bg1_ref, bg2_ref,
                  wb_ref, o_ref):
    h = h_ref[...]
    acc = None
    for n, (br_ref, wg_ref, bg_ref) in enumerate(((oa_ref, wg0_ref, bg0_ref), (ob_ref, wg1_ref, bg1_ref),
                                                  (oc_ref, wg2_ref, bg2_ref))):
        gate = jax.nn.sigmoid(_dot(h, wg_ref[0]) + bg_ref[0])
        term = gate * _dot(br_ref[...], wb_ref[n])
        acc = term if acc is None else acc + term
    o_ref[...] = acc.astype(BF16)


def _gated_merge(h, oa, ob, oc, w_gate, b_gate, w_branch, layer):
    m, d = h.shape
    bm = _pick(m, 1024, 8)
    bn = _pick(d, 256, LANES)
    nj = d // bn
    once = pl.Buffered(1)
    br = pl.BlockSpec((bm, BRANCH_W), lambda i, j: (i, 0), pipeline_mode=once)
    wg = [pl.BlockSpec((1, d, bn), lambda i, j, n=n: (0, 0, n * nj + j)) for n in range(N_BRANCH)]
    bg = [pl.BlockSpec((1, 1, bn), lambda i, j, n=n: (layer, 0, n * nj + j)) for n in range(N_BRANCH)]
    return pl.pallas_call(
        _merge_kernel,
        grid=(m // bm, nj),
        in_specs=[pl.BlockSpec((bm, d), lambda i, j: (i, 0), pipeline_mode=once), br, br, br, *wg, *bg,
                  pl.BlockSpec((N_BRANCH, BRANCH_W, bn), lambda i, j: (0, 0, j))],
        out_specs=pl.BlockSpec((bm, bn), lambda i, j: (i, j)),
        out_shape=jax.ShapeDtypeStruct((m, d), BF16),
        compiler_params=_params(("arbitrary",) * 2),
        name="gated_merge",
    )(h, oa, ob, oc, w_gate, w_gate, w_gate, b_gate, b_gate, b_gate, w_branch)


def _conv_act_kernel(a_ref, g_ref, w_ref, b_ref, o_ref):
    a = a_ref[0].astype(F32)
    s = a.shape[0]
    row = lax.broadcasted_iota(jnp.int32, a.shape, 0)
    prev = jnp.where(row == 0, 0.0, pltpu.roll(a, 1, 0))
    nxt = jnp.where(row == s - 1, 0.0, pltpu.roll(a, s - 1, 0))
    w = w_ref[...]
    z = prev * w[0:1] + a * w[1:2] + nxt * w[2:3] + b_ref[...]
    gelu = 0.5 * z * (1.0 + lax.erf(z * math.sqrt(0.5)))
    o_ref[0] = (gelu * g_ref[0].astype(F32)).astype(BF16)


def _conv_act(u, conv_w, conv_b, batch, s):
    ffp = conv_w.shape[1]
    bc = _pick(ffp, 256, LANES)
    nj = ffp // bc
    u3 = u.reshape(batch, s, 2 * ffp)
    out = pl.pallas_call(
        _conv_act_kernel,
        grid=(batch, nj),
        in_specs=[pl.BlockSpec((1, s, bc), lambda b, j: (b, 0, j)),
                  pl.BlockSpec((1, s, bc), lambda b, j: (b, 0, nj + j)),
                  pl.BlockSpec((3, bc), lambda b, j: (0, j)),
                  pl.BlockSpec((1, bc), lambda b, j: (0, j))],
        out_specs=pl.BlockSpec((1, s, bc), lambda b, j: (b, 0, j)),
        out_shape=jax.ShapeDtypeStruct((batch, s, ffp), BF16),
        compiler_params=_params(("arbitrary", "arbitrary")),
        name="conv_act",
    )(u3, u3, conv_w, conv_b)
    return out.reshape(batch * s, ffp)


def _rope_tables(n):
    n_rows = n // GRID_W
    rows = jnp.broadcast_to(jnp.arange(n_rows)[:, None], (n_rows, GRID_W)).reshape(-1).astype(F32)
    cols = jnp.broadcast_to(jnp.arange(GRID_W)[None, :], (n_rows, GRID_W)).reshape(-1).astype(F32)
    quarter = D_HEAD // 4
    inv_freq = ROPE_THETA ** (-jnp.arange(quarter, dtype=F32) / quarter)
    ang = jnp.concatenate([rows[:, None] * inv_freq, cols[:, None] * inv_freq], axis=-1)
    ang = jnp.concatenate([ang, ang], axis=-1)
    sign = jnp.where(jnp.arange(D_HEAD) < D_HEAD // 2, -1.0, 1.0).astype(F32)
    return jnp.cos(ang), jnp.sin(ang) * sign


def kernel(x, c, ctx, c_ctx, w_ada, b_ada, w_in, diff_lambda, diff_subln_g, qk_norm_g, w_branch, w_gate,
           b_gate, w_o, ln1_g, ln1_b, w_up, conv_w, conv_b, w_down, ln2_g, ln2_b):
    batch, seq, d = x.shape
    n_ctx = ctx.shape[1]
    depth = w_ada.shape[0]
    ff = conv_w.shape[2]
    assert ff % LANES == 0 and w_in.shape[2] == D_IN
    ffp = -(-ff // 1024) * 1024 if ff > 1024 else ff
    alpha = (2 * depth) ** 0.25

    cvec = jnp.zeros((8, d), F32).at[:batch].set(c).at[batch].set(c_ctx)
    mod = _ada(cvec, w_ada, b_ada).reshape(depth, 8, 6, d)

    pad = ffp - ff
    conv_w_p = jnp.pad(conv_w, ((0, 0), (0, 0), (0, pad)))
    conv_b_p = jnp.pad(conv_b, ((0, 0), (0, pad))).reshape(depth, 1, ffp)
    b_gate3 = b_gate.reshape(depth, 1, N_BRANCH * d)
    w_branch2 = w_branch.reshape(depth, N_BRANCH * BRANCH_W, d)

    _, inproj_grid = _inproj_grid(batch * seq, seq)
    n_steps = inproj_grid[0] * inproj_grid[1]

    def layer_casts(l):
        casts = [_Cast(w_gate, l, n_steps), _Cast(w_branch2, l, n_steps), _Cast(w_o, l, n_steps),
                 _Cast(w_up, l, n_steps, nseg=2, cp=ffp), _Cast(w_down, l, n_steps, rp=ffp)]
        if l + 1 < depth:
            casts.append(_Cast(w_in, l + 1, n_steps))
        return casts

    w_in_b = _cast_now(_Cast(w_in, 0, 32))

    cos_l, sin_l = _rope_tables(seq)
    cos_c, sin_c = jnp.ones((n_ctx, D_HEAD), F32), jnp.zeros((n_ctx, D_HEAD), F32)
    pos_l, pos_c = _dft_tables(seq // FNET_RADIX), _dft_tables(n_ctx // FNET_RADIX)
    tw_l, tw_c = _fnet_twiddles(seq), _fnet_twiddles(n_ctx)
    chan = _dft_tables(FNET_GROUP_W)

    xl = x.reshape(batch * seq, d)
    xc = ctx.reshape(batch * n_ctx, d)
    m_ctx = batch * n_ctx
    bq_diff_l = _pick(seq, 4 * ATTN_ROWS, 8)
    bq_gqa_l = _pick(seq, 2 * ATTN_ROWS, 8)
    bq_c = _pick(n_ctx, ATTN_ROWS, 8)
    bk_ff = _pick(ffp, 2816, LANES)

    def lat(i, l):
        return mod[l, :batch, i][:, None, :]

    def cx(i, l):
        return mod[l, batch, i][None, None, :]

    hl = _modulate(xl, lat(1, 0), lat(0, 0), seq)
    hc = _modulate(xc, cx(1, 0), cx(0, 0), m_ctx)

    for l in range(depth):
        last = l == depth - 1
        lam_init = 0.8 - 0.6 * math.exp(-0.3 * l)

        pl_, w_gate_b, w_branch_b, w_o_b, w_up_b, w_down_b, *w_in_next = _inproj(
            hl, w_in_b, cos_l, sin_l, qk_norm_g[l], seq, casts=layer_casts(l))
        w_branch_b = w_branch_b.reshape(N_BRANCH, BRANCH_W, d)
        (pc_,) = _inproj(hc, w_in_b, cos_c, sin_c, qk_norm_g[l], n_ctx)
        kv = [(pl_, seq), (pc_, n_ctx)]
        oa = _diff_attention(pl_, kv, diff_lambda[l], diff_subln_g[l], lam_init, batch, seq, bq_diff_l)
        ob = _fourier_mix(pl_, batch, seq, pos_l, tw_l, chan)
        oc = _gqa_attention(pl_, kv, batch, seq, bq_gqa_l)
        merged = _gated_merge(hl, oa, ob, oc, w_gate_b, b_gate3, w_branch_b, l)
        yl = _matmul_residual(merged, w_o_b, xl, lat(2, l), seq, alpha, bk=d, name="out_proj")
        xl, hl = _ln(yl, ln1_g[l], ln1_b[l], mod=(lat(4, l), lat(3, l)), rows_per_group=seq)

        if not last:
            kvc = [(pc_, n_ctx)]
            oa_c = _diff_attention(pc_, kvc, diff_lambda[l], diff_subln_g[l], lam_init, batch, n_ctx, bq_c)
            ob_c = _fourier_mix(pc_, batch, n_ctx, pos_c, tw_c, chan)
            oc_c = _gqa_attention(pc_, kvc, batch, n_ctx, bq_c)
            merged_c = _gated_merge(hc, oa_c, ob_c, oc_c, w_gate_b, b_gate3, w_branch_b, l)
            yc = _matmul_residual(merged_c, w_o_b, xc, cx(2, l), m_ctx, alpha, bk=d, name="out_proj_ctx")
            xc, hc = _ln(yc, ln1_g[l], ln1_b[l], mod=(cx(4, l), cx(3, l)), rows_per_group=m_ctx)

        def ffn(h, xres, gate, rows_per_group, s):
            mrows = h.shape[0]
            u = _matmul(h, w_up_b, bm=_pick(rows_per_group, 1024, 8), bn=_pick(2 * ffp, 1024, LANES), bk=d,
                        out_dtype=BF16, epilogue=_identity, name="ffn_up")
            act = _conv_act(u, conv_w_p[l], conv_b_p[l], mrows // s, s)
            return _matmul_residual(act, w_down_b, xres, gate, rows_per_group, alpha, bk=bk_ff,
                                    name="ffn_down")

        yl = ffn(hl, xl, lat(5, l), seq, seq)
        if last:
            xl = _ln(yl, ln2_g[l], ln2_b[l])
        else:
            xl, hl = _ln(yl, ln2_g[l], ln2_b[l], mod=(lat(1, l + 1), lat(0, l + 1)), rows_per_group=seq)
            yc = ffn(hc, xc, cx(5, l), m_ctx, n_ctx)
            xc, hc = _ln(yc, ln2_g[l], ln2_b[l], mod=(cx(1, l + 1), cx(0, l + 1)), rows_per_group=m_ctx)

        if not last:
            (w_in_b,) = w_in_next

    return xl.reshape(batch, seq, d)
```

```python
import functools
import math

import jax
import jax.numpy as jnp
from jax import lax
from jax.experimental import pallas as pl
from jax.experimental.pallas import tpu as pltpu

F32 = jnp.float32
BF16 = jnp.bfloat16

GRID_W = 64
D_HEAD = 128
H_DIFF = 8
DIFF_W = 2 * D_HEAD
FNET_GROUPS = 8
FNET_GROUP_W = 256
H_GQA = 16
H_KV = 4
GQA_GROUP = H_GQA // H_KV
GQA_W = GQA_GROUP * D_HEAD
BRANCH_W = 2048
N_BRANCH = 3
ROPE_THETA = 10000.0
NORM_EPS = 1e-6

OFF_AQ = 0
OFF_AK = OFF_AQ + H_DIFF * DIFF_W
OFF_AV = OFF_AK + H_DIFF * DIFF_W
OFF_B = OFF_AV + H_DIFF * DIFF_W
OFF_CQ = OFF_B + FNET_GROUPS * FNET_GROUP_W
OFF_CK = OFF_CQ + H_GQA * D_HEAD
OFF_CV = OFF_CK + H_KV * D_HEAD
D_IN = OFF_CV + H_KV * D_HEAD
Q_SCALE = D_HEAD ** -0.5 * math.log2(math.e)

V7X_VMEM_LIMIT_BYTES = 56 * 1024 * 1024
LANES = 128
IN_TILE = 512
ATTN_ROWS = 256


def _params(sem):
    return pltpu.CompilerParams(dimension_semantics=sem, vmem_limit_bytes=V7X_VMEM_LIMIT_BYTES)


def _pick(dim, pref, align):
    t = (min(pref, dim) // align) * align
    while t >= align:
        if dim % t == 0:
            return t
        t -= align
    return dim


def _dot(a, b):
    return jnp.dot(a, b, preferred_element_type=F32)


def _dot_nt(a, b):
    return lax.dot_general(a, b, (((1,), (1,)), ((), ())), preferred_element_type=F32)


def _pipelined(n_chains, issue, finish, lookahead=1):
    pending = [issue(c) for c in range(min(lookahead, n_chains))]
    for c in range(n_chains):
        if c + lookahead < n_chains:
            pending.append(issue(c + lookahead))
        finish(c, pending.pop(0))


class _Cast:
    def __init__(self, w, layer, max_blocks, *, nseg=1, cp=None, rp=None):
        _, r, c_all = w.shape
        self.w, self.layer, self.nseg = w, layer, nseg
        self.c = c_all // nseg
        self.cp = cp or self.c
        rows_out = rp or r
        div = math.gcd(r, rows_out)
        self.rb = next(rb for rb in range(16, div + 1, 16) if div % rb == 0 and rows_out // rb <= max_blocks)
        self.n_in, self.n_out = r // self.rb, rows_out // self.rb
        self.out_shape = jax.ShapeDtypeStruct((1, rows_out, nseg * self.cp), BF16)

    def specs(self, step):
        out_blk = lambda *g: jnp.minimum(step(*g), self.n_out - 1)
        return (pl.BlockSpec((1, self.rb, self.nseg * self.c),
                             lambda *g: (self.layer, jnp.minimum(out_blk(*g), self.n_in - 1), 0)),
                pl.BlockSpec((1, self.rb, self.nseg * self.cp), lambda *g: (0, out_blk(*g), 0)))

    def body(self, t, x_ref, o_ref):
        x = x_ref[0]
        if self.n_out > self.n_in:
            x = jnp.where(jnp.minimum(t, self.n_out - 1) < self.n_in, x, 0.0)
        for s in range(self.nseg):
            o_ref[0, :, s * self.cp:s * self.cp + self.c] = x[:, s * self.c:(s + 1) * self.c].astype(BF16)
            if self.cp > self.c:
                o_ref[0, :, s * self.cp + self.c:(s + 1) * self.cp] = jnp.zeros((self.rb, self.cp - self.c), BF16)


def _run_casts(t, casts, refs):
    for n, cast in enumerate(casts):
        cast.body(t, refs[n], refs[len(casts) + n])


def _cast_kernel(x_ref, o_ref, *, cast):
    cast.body(pl.program_id(0), x_ref, o_ref)


def _cast_now(cast):
    in_spec, out_spec = cast.specs(lambda t: t)
    return pl.pallas_call(
        functools.partial(_cast_kernel, cast=cast), grid=(cast.n_out,), in_specs=[in_spec],
        out_specs=out_spec, out_shape=cast.out_shape,
        compiler_params=_params(("arbitrary",)), name="cast_weight",
    )(cast.w)


def _bf16_pieces(x, n):
    pieces = []
    for _ in range(n):
        p = x.astype(BF16).astype(F32)
        pieces.append(p)
        x = x - p
    return pieces


def _ada_kernel(c_ref, w_ref, b_ref, o_ref):
    c = c_ref[...]
    rows = c.shape[0]
    s3 = jnp.concatenate(_bf16_pieces(c * jax.nn.sigmoid(c), 3), axis=0).astype(BF16)
    w_hi, w_lo = _bf16_pieces(w_ref[0], 2)
    hi = _dot(s3, w_hi.astype(BF16))
    lo = _dot(s3, w_lo.astype(BF16))
    acc = (hi[:rows] + hi[rows:2 * rows] + hi[2 * rows:]) + (lo[:rows] + lo[rows:2 * rows])
    o_ref[0] = acc + b_ref[0]


def _ada(cvec, w_ada, b_ada):
    depth, d, n = w_ada.shape
    bn = _pick(n, 512, LANES)
    return pl.pallas_call(
        _ada_kernel,
        grid=(depth, n // bn),
        in_specs=[pl.BlockSpec((8, d), lambda l, j: (0, 0)),
                  pl.BlockSpec((1, d, bn), lambda l, j: (l, 0, j)),
                  pl.BlockSpec((1, 1, bn), lambda l, j: (l, 0, j))],
        out_specs=pl.BlockSpec((1, 8, bn), lambda l, j: (l, 0, j)),
        out_shape=jax.ShapeDtypeStruct((depth, 8, n), F32),
        compiler_params=_params(("arbitrary", "arbitrary")),
        name="ada",
    )(cvec, w_ada, b_ada.reshape(depth, 1, n))


def _mod_kernel(x_ref, sc_ref, sh_ref, h_ref):
    h_ref[...] = (x_ref[...] * (1.0 + sc_ref[0]) + sh_ref[0]).astype(BF16)


def _modulate(x, sc, sh, rows_per_group):
    m, d = x.shape
    bs = _pick(rows_per_group, 256, 8)
    grp = lambda i: (i * bs // rows_per_group, 0, 0)
    return pl.pallas_call(
        _mod_kernel,
        grid=(m // bs,),
        in_specs=[pl.BlockSpec((bs, d), lambda i: (i, 0)),
                  pl.BlockSpec((1, 1, d), grp),
                  pl.BlockSpec((1, 1, d), grp)],
        out_specs=pl.BlockSpec((bs, d), lambda i: (i, 0)),
        out_shape=jax.ShapeDtypeStruct((m, d), BF16),
        compiler_params=_params(("arbitrary",)),
        name="modulate",
    )(x, sc, sh)


def _layer_norm(y, g, b):
    mu = jnp.mean(y, axis=-1, keepdims=True)
    dlt = y - mu
    var = jnp.mean(dlt * dlt, axis=-1, keepdims=True)
    return dlt * lax.rsqrt(var + NORM_EPS) * g + b


def _ln_mod_kernel(y_ref, g_ref, b_ref, sc_ref, sh_ref, x_ref, h_ref):
    xn = _layer_norm(y_ref[...], g_ref[...], b_ref[...])
    x_ref[...] = xn
    h_ref[...] = (xn * (1.0 + sc_ref[0]) + sh_ref[0]).astype(BF16)


def _ln_kernel(y_ref, g_ref, b_ref, x_ref):
    x_ref[...] = _layer_norm(y_ref[...], g_ref[...], b_ref[...])


def _ln(y, g, b, mod=None, rows_per_group=None):
    m, d = y.shape
    bs = _pick(rows_per_group or m, 256, 8)
    row = pl.BlockSpec((bs, d), lambda i: (i, 0))
    vec = pl.BlockSpec((1, d), lambda i: (0, 0))
    g2, b2 = g.reshape(1, d), b.reshape(1, d)
    if mod is None:
        return pl.pallas_call(
            _ln_kernel, grid=(m // bs,), in_specs=[row, vec, vec], out_specs=row,
            out_shape=jax.ShapeDtypeStruct((m, d), F32),
            compiler_params=_params(("arbitrary",)), name="layernorm",
        )(y, g2, b2)
    grp = pl.BlockSpec((1, 1, d), lambda i: (i * bs // rows_per_group, 0, 0))
    return pl.pallas_call(
        _ln_mod_kernel, grid=(m // bs,), in_specs=[row, vec, vec, grp, grp],
        out_specs=[row, row],
        out_shape=[jax.ShapeDtypeStruct((m, d), F32), jax.ShapeDtypeStruct((m, d), BF16)],
        compiler_params=_params(("arbitrary",)), name="layernorm_mod",
    )(y, g2, b2, mod[0], mod[1])


def _mm_kernel(*refs, nk, n_extra, epilogue):
    x_ref, w_ref = refs[0], refs[1]
    extra = refs[2:2 + n_extra]
    o_ref = refs[2 + n_extra]
    part = _dot(x_ref[...], w_ref[0])
    if nk == 1:
        o_ref[...] = epilogue(part, *extra).astype(o_ref.dtype)
        return
    acc_ref = refs[3 + n_extra]
    k = pl.program_id(2)

    @pl.when(k == 0)
    def _():
        acc_ref[...] = part

    @pl.when((k > 0) & (k < nk - 1))
    def _():
        acc_ref[...] += part

    @pl.when(k == nk - 1)
    def _():
        o_ref[...] = epilogue(acc_ref[...] + part, *extra).astype(o_ref.dtype)


def _matmul(x, w, *, bm, bn, bk, out_dtype, epilogue, extra=(), extra_specs=(), name):
    m, kdim = x.shape
    n = w.shape[2]
    nk = kdim // bk
    kernel = functools.partial(_mm_kernel, nk=nk, n_extra=len(extra), epilogue=epilogue)
    return pl.pallas_call(
        kernel,
        grid=(m // bm, n // bn, nk),
        in_specs=[pl.BlockSpec((bm, bk), lambda i, j, k: (i, k)),
                  pl.BlockSpec((1, bk, bn), lambda i, j, k: (0, k, j)),
                  *extra_specs],
        out_specs=pl.BlockSpec((bm, bn), lambda i, j, k: (i, j)),
        out_shape=jax.ShapeDtypeStruct((m, n), out_dtype),
        scratch_shapes=[pltpu.VMEM((bm, bn), F32)] if nk > 1 else [],
        compiler_params=_params(("arbitrary", "arbitrary", "arbitrary")),
        name=name,
    )(x, w, *extra)


def _identity(acc):
    return acc


def _residual_epilogue(acc, x_ref, gate_ref, *, alpha):
    return alpha * x_ref[...] + gate_ref[0] * acc


def _matmul_residual(a, w, xres, gate, rows_per_group, alpha, *, bk, name):
    n = w.shape[2]
    bm = _pick(rows_per_group, 1024, 8)
    bn = _pick(n, 1024, LANES)
    return _matmul(
        a, w, bm=bm, bn=bn, bk=bk, out_dtype=F32,
        epilogue=functools.partial(_residual_epilogue, alpha=alpha),
        extra=(xres, gate),
        extra_specs=(pl.BlockSpec((bm, bn), lambda i, j, k: (i, j)),
                     pl.BlockSpec((1, 1, bn), lambda i, j, k: (i * bm // rows_per_group, 0, j))),
        name=name)


def _rope(x, cos, sin_signed):
    return x * cos + pltpu.roll(x, D_HEAD // 2, 1) * sin_signed


def _inproj_kernel(h_ref, w_ref, cos_ref, sin_ref, g_ref, *refs, casts, n_col_tiles):
    o_ref = refs[len(casts)]
    i, j = pl.program_id(0), pl.program_id(1)
    _run_casts(i * n_col_tiles + j, casts, refs[:len(casts)] + refs[len(casts) + 1:])
    acc = _dot(h_ref[...], w_ref[0])
    n_chunks = IN_TILE // D_HEAD
    t_ak, t_av, t_cq, t_ck, t_cv = (OFF_AK // IN_TILE, OFF_AV // IN_TILE, OFF_CQ // IN_TILE,
                                    OFF_CK // IN_TILE, OFF_CV // IN_TILE)

    @pl.when(j < t_av)
    def _():
        s = jnp.where(j < t_ak, Q_SCALE, 1.0).astype(F32)
        cos = cos_ref[...] * s
        sin = sin_ref[...] * s
        for c in range(n_chunks):
            sl = slice(c * D_HEAD, (c + 1) * D_HEAD)
            o_ref[:, sl] = _rope(acc[:, sl], cos, sin).astype(BF16)

    @pl.when(((j >= t_av) & (j < t_cq)) | (j >= t_cv))
    def _():
        o_ref[...] = acc.astype(BF16)

    @pl.when((j >= t_cq) & (j < t_cv))
    def _():
        is_q = j < t_ck
        g = jnp.where(is_q, g_ref[0:1, :], g_ref[1:2, :])
        s = jnp.where(is_q, Q_SCALE, 1.0).astype(F32)
        cos = cos_ref[...] * s
        sin = sin_ref[...] * s
        for c in range(n_chunks):
            sl = slice(c * D_HEAD, (c + 1) * D_HEAD)
            x = acc[:, sl]
            ms = jnp.mean(x * x, axis=-1, keepdims=True)
            xn = x * lax.rsqrt(ms + NORM_EPS) * g
            o_ref[:, sl] = _rope(xn, cos, sin).astype(BF16)


def _inproj_grid(m, rows_per_seq):
    bm = _pick(rows_per_seq, 1024, 8)
    return bm, (m // bm, D_IN // IN_TILE)


def _inproj(h, w, cos, sin_signed, qk_g, rows_per_seq, casts=()):
    m, d = h.shape
    bm, grid = _inproj_grid(m, rows_per_seq)
    nseq_blk = rows_per_seq // bm
    tab = pl.BlockSpec((bm, D_HEAD), lambda i, j: (i % nseq_blk, 0))
    cast_specs = [cast.specs(lambda i, j: i * grid[1] + j) for cast in casts]
    return pl.pallas_call(
        functools.partial(_inproj_kernel, casts=tuple(casts), n_col_tiles=grid[1]),
        grid=grid,
        in_specs=[pl.BlockSpec((bm, d), lambda i, j: (i, 0)),
                  pl.BlockSpec((1, d, IN_TILE), lambda i, j: (0, 0, j)),
                  tab, tab,
                  pl.BlockSpec((2, D_HEAD), lambda i, j: (0, 0)),
                  *[sp[0] for sp in cast_specs]],
        out_specs=[pl.BlockSpec((bm, IN_TILE), lambda i, j: (i, j)), *[sp[1] for sp in cast_specs]],
        out_shape=[jax.ShapeDtypeStruct((m, D_IN), BF16), *[cast.out_shape for cast in casts]],
        compiler_params=_params(("arbitrary", "arbitrary")),
        name="inproj",
    )(h, w, cos, sin_signed, qk_g, *[cast.w for cast in casts])


KEY_CHUNK = 1024


class _SoftmaxChain:
    def __init__(self, q, k_refs, v_refs, k_lo, v_lo, v_width):
        self.q, self.k_refs, self.v_refs = q, k_refs, v_refs
        self.k_lo, self.v_lo, self.v_width = k_lo, v_lo, v_width
        self.chunks = [(n, c0, min(c0 + KEY_CHUNK, k_ref.shape[0]))
                       for n, k_ref in enumerate(k_refs) for c0 in range(0, k_ref.shape[0], KEY_CHUNK)]
        self.scores = []
        self.lane_max = self.lane_sum = self.acc = self.row_max = None

    def score(self, i):
        n, c0, c1 = self.chunks[i]
        s = _dot_nt(self.q, self.k_refs[n][c0:c1, self.k_lo:self.k_lo + D_HEAD])
        self.scores.append(s)
        for t in range(0, c1 - c0, LANES):
            tile = s[:, t:t + LANES]
            self.lane_max = tile if self.lane_max is None else jnp.maximum(self.lane_max, tile)

    def accumulate(self, i):
        if self.row_max is None:
            self.row_max = jnp.max(self.lane_max, axis=-1, keepdims=True)
        n, c0, c1 = self.chunks[i]
        e = jnp.exp2(self.scores[i] - self.row_max)
        for t in range(0, c1 - c0, LANES):
            tile = e[:, t:t + LANES]
            self.lane_sum = tile if self.lane_sum is None else self.lane_sum + tile
        pv = _dot(e.astype(BF16), self.v_refs[n][c0:c1, self.v_lo:self.v_lo + self.v_width])
        self.acc = pv if self.acc is None else self.acc + pv

    def result(self):
        return self.acc, jnp.sum(self.lane_sum, axis=-1, keepdims=True)


def _run_chains(n_blocks, make_chains, write):
    cur = make_chains(0)
    for i in range(len(cur[0].chunks)):
        for ch in cur:
            ch.score(i)
    for b in range(n_blocks):
        nxt = make_chains(b + 1) if b + 1 < n_blocks else []
        for i in range(len(cur[0].chunks)):
            for ch in nxt:
                ch.score(i)
            for ch in cur:
                ch.accumulate(i)
        write(b, cur)
        cur = nxt


def _diff_attn_kernel(*refs, n_src, lam_init):
    q_ref = refs[0]
    k_refs = refs[1:1 + n_src]
    v_refs = refs[1 + n_src:1 + 2 * n_src]
    lam_ref, g_ref, o_ref = refs[1 + 2 * n_src:]
    lp = lam_ref[...]
    lam = (jnp.exp(jnp.sum(lp[0:1] * lp[1:2], axis=-1, keepdims=True))
           - jnp.exp(jnp.sum(lp[2:3] * lp[3:4], axis=-1, keepdims=True)) + lam_init)
    rows = min(ATTN_ROWS, q_ref.shape[0])

    def make_chains(b):
        q = q_ref[b * rows:(b + 1) * rows, :]
        return [_SoftmaxChain(q[:, lo:lo + D_HEAD], k_refs, v_refs, lo, 0, DIFF_W) for lo in (0, D_HEAD)]

    def write(b, chains):
        (o1, den1), (o2, den2) = chains[0].result(), chains[1].result()
        o = o1 * (1.0 / den1) - o2 * (lam / den2)
        ms = jnp.mean(o * o, axis=-1, keepdims=True)
        y = (o * lax.rsqrt(ms + NORM_EPS)) * g_ref[...] * (1.0 - lam_init)
        o_ref[b * rows:(b + 1) * rows, :] = y.astype(BF16)

    _run_chains(q_ref.shape[0] // rows, make_chains, write)


def _diff_attention(pq, kv_sources, lam_p, subln_g, lam_init, batch, q_len, bq):
    nq = q_len // bq
    kblk, vblk = OFF_AK // DIFF_W, OFF_AV // DIFF_W
    k_specs = [pl.BlockSpec((n, DIFF_W), lambda b, h, i: (b, kblk + h)) for _, n in kv_sources]
    v_specs = [pl.BlockSpec((n, DIFF_W), lambda b, h, i: (b, vblk + h)) for _, n in kv_sources]
    srcs = [p for p, _ in kv_sources]
    kernel = functools.partial(_diff_attn_kernel, n_src=len(srcs), lam_init=lam_init)
    return pl.pallas_call(
        kernel,
        grid=(batch, H_DIFF, nq),
        in_specs=[pl.BlockSpec((bq, DIFF_W), lambda b, h, i: (b * nq + i, h)),
                  *k_specs, *v_specs,
                  pl.BlockSpec((4, D_HEAD), lambda b, h, i: (0, 0)),
                  pl.BlockSpec((1, DIFF_W), lambda b, h, i: (0, 0))],
        out_specs=pl.BlockSpec((bq, DIFF_W), lambda b, h, i: (b * nq + i, h)),
        out_shape=jax.ShapeDtypeStruct((batch * q_len, H_DIFF * DIFF_W), BF16),
        compiler_params=_params(("arbitrary", "arbitrary", "arbitrary")),
        name="diff_attention",
    )(pq, *srcs, *srcs, lam_p, subln_g.reshape(1, DIFF_W))


def _gqa_kernel(*refs, n_src):
    q_ref = refs[0]
    k_refs = refs[1:1 + n_src]
    v_refs = refs[1 + n_src:1 + 2 * n_src]
    o_ref = refs[1 + 2 * n_src]
    rows = min(ATTN_ROWS // 2, q_ref.shape[0])
    heads = [slice(g * D_HEAD, (g + 1) * D_HEAD) for g in range(GQA_GROUP)]

    def make_chains(b):
        rs = slice(b * rows, (b + 1) * rows)
        q = jnp.concatenate([q_ref[rs, cs] for cs in heads], axis=0)
        return [_SoftmaxChain(q, k_refs, v_refs, 0, 0, D_HEAD)]

    def write(b, chains):
        o, den = chains[0].result()
        o = (o / den).astype(BF16)
        for g, cs in enumerate(heads):
            o_ref[b * rows:(b + 1) * rows, cs] = o[g * rows:(g + 1) * rows]

    _run_chains(q_ref.shape[0] // rows, make_chains, write)


def _gqa_attention(pq, kv_sources, batch, q_len, bq):
    nq = q_len // bq
    qblk, kblk, vblk = OFF_CQ // GQA_W, OFF_CK // D_HEAD, OFF_CV // D_HEAD
    k_specs = [pl.BlockSpec((n, D_HEAD), lambda b, kh, i: (b, kblk + kh)) for _, n in kv_sources]
    v_specs = [pl.BlockSpec((n, D_HEAD), lambda b, kh, i: (b, vblk + kh)) for _, n in kv_sources]
    srcs = [p for p, _ in kv_sources]
    return pl.pallas_call(
        functools.partial(_gqa_kernel, n_src=len(srcs)),
        grid=(batch, H_KV, nq),
        in_specs=[pl.BlockSpec((bq, GQA_W), lambda b, kh, i: (b * nq + i, qblk + kh)),
                  *k_specs, *v_specs],
        out_specs=pl.BlockSpec((bq, GQA_W), lambda b, kh, i: (b * nq + i, kh)),
        out_shape=jax.ShapeDtypeStruct((batch * q_len, H_GQA * D_HEAD), BF16),
        compiler_params=_params(("arbitrary",) * 3),
        name="gqa_attention",
    )(pq, *srcs, *srcs)


def _dft_tables(n, split=64):
    k = jnp.arange(n, dtype=jnp.int32)

    def cos_sin(t):
        ang = ((k[:, None] * t[None, :]) % n).astype(F32) * (2.0 * math.pi / n)
        return jnp.cos(ang), jnp.sin(ang)

    if n <= split or n % split:
        c, s = cos_sin(k)
    else:
        ca, sa = (v[:, :, None] for v in cos_sin(jnp.arange(0, n, split, dtype=jnp.int32)))
        cb, sb = (v[:, None, :] for v in cos_sin(jnp.arange(split, dtype=jnp.int32)))
        c = (ca * cb - sa * sb).reshape(n, n)
        s = (sa * cb + ca * sb).reshape(n, n)
    return c.astype(BF16), s.astype(BF16)


FNET_RADIX = 4


def _fnet_twiddles(n):
    p = jnp.arange(1, FNET_RADIX, dtype=jnp.int32)[:, None]
    k = jnp.arange(n // FNET_RADIX, dtype=jnp.int32)[None, :]
    ang = ((p * k) % n).astype(F32) * (2.0 * math.pi / n)
    return jnp.cos(ang)[:, :, None], jnp.sin(ang)[:, :, None]


def _fnet_kernel(c_ref, s_ref, x0_ref, x1_ref, x2_ref, x3_ref, twc_ref, tws_ref, cc_ref, sc_ref, o_ref, *, scale):
    c, s = c_ref[...], s_ref[...]
    ys = []
    for p, x_ref in enumerate((x0_ref, x1_ref, x2_ref, x3_ref)):
        x = x_ref[...]
        zr, zi = _dot(c, x), -_dot(s, x)
        if p:
            tc, ts = twc_ref[p - 1], tws_ref[p - 1]
            zr, zi = zr * tc + zi * ts, zi * tc - zr * ts
        ys.append((zr, zi))
    (ar, ai), (br, bi), (cr, ci), (dr, di) = ys
    zs = ((ar + br + cr + dr, ai + bi + ci + di),
          (ar + bi - cr - di, ai - br - ci + dr),
          (ar - br + cr - dr, ai - bi + ci - di),
          (ar - bi - cr + di, ai + br - ci - dr))
    for q, (zr, zi) in enumerate(zs):
        zr, zi = zr.astype(BF16), zi.astype(BF16)
        for g in range(zr.shape[1] // FNET_GROUP_W):
            sl = slice(g * FNET_GROUP_W, (g + 1) * FNET_GROUP_W)
            y = _dot(zr[:, sl], cc_ref[...]) + _dot(zi[:, sl], sc_ref[...])
            o_ref[0, q, :, sl] = (y * scale).astype(BF16)


def _fourier_mix(p, batch, n, pos_tabs, twiddles, chan_tabs):
    nr = n // FNET_RADIX
    bk = _pick(nr, 512, 16)
    bc = 2 * FNET_GROUP_W
    width = FNET_GROUPS * FNET_GROUP_W
    pos = pl.BlockSpec((bk, nr), lambda b, i, j: (i, 0))
    tw = pl.BlockSpec((FNET_RADIX - 1, bk, 1), lambda b, i, j: (0, i, 0))
    chan = pl.BlockSpec((FNET_GROUP_W, FNET_GROUP_W), lambda b, i, j: (0, 0))
    p4 = p[:, OFF_B:OFF_B + width].reshape(batch * nr, FNET_RADIX * width)
    xs = [pl.BlockSpec((nr, bc), lambda b, i, j, r=r: (b, r * width // bc + j)) for r in range(FNET_RADIX)]
    scale = 1.0 / math.sqrt(n * FNET_GROUP_W)
    out = pl.pallas_call(
        functools.partial(_fnet_kernel, scale=scale),
        grid=(batch, nr // bk, width // bc),
        in_specs=[pos, pos, *xs, tw, tw, chan, chan],
        out_specs=pl.BlockSpec((1, FNET_RADIX, bk, bc), lambda b, i, j: (b, 0, i, j)),
        out_shape=jax.ShapeDtypeStruct((batch, FNET_RADIX, nr, width), BF16),
        compiler_params=_params(("arbitrary",) * 3),
        name="fourier_mix",
    )(*pos_tabs, p4, p4, p4, p4, *twiddles, *chan_tabs)
    return out.reshape(batch * n, width)


def _merge_kernel(h_ref, oa_ref, ob_ref, oc_ref, wg0_ref, wg1_ref, wg2_ref, bg0_ref, bg1_ref, bg2_ref,
                  wb_ref, o_ref):
    h = h_ref[...]
    acc = None
    for n, (br_ref, wg_ref, bg_ref) in enumerate(((oa_ref, wg0_ref, bg0_ref), (ob_ref, wg1_ref, bg1_ref),
                                                  (oc_ref, wg2_ref, bg2_ref))):
        gate = jax.nn.sigmoid(_dot(h, wg_ref[0]) + bg_ref[0])
        term = gate * _dot(br_ref[...], wb_ref[n])
        acc = term if acc is None else acc + term
    o_ref[...] = acc.astype(BF16)


def _gated_merge(h, oa, ob, oc, w_gate, b_gate, w_branch, layer):
    m, d = h.shape
    bm = _pick(m, 1024, 8)
    bn = _pick(d, 256, LANES)
    nj = d // bn
    once = pl.Buffered(1)
    br = pl.BlockSpec((bm, BRANCH_W), lambda i, j: (i, 0), pipeline_mode=once)
    wg = [pl.BlockSpec((1, d, bn), lambda i, j, n=n: (0, 0, n * nj + j)) for n in range(N_BRANCH)]
    bg = [pl.BlockSpec((1, 1, bn), lambda i, j, n=n: (layer, 0, n * nj + j)) for n in range(N_BRANCH)]
    return pl.pallas_call(
        _merge_kernel,
        grid=(m // bm, nj),
        in_specs=[pl.BlockSpec((bm, d), lambda i, j: (i, 0), pipeline_mode=once), br, br, br, *wg, *bg,
                  pl.BlockSpec((N_BRANCH, BRANCH_W, bn), lambda i, j: (0, 0, j))],
        out_specs=pl.BlockSpec((bm, bn), lambda i, j: (i, j)),
        out_shape=jax.ShapeDtypeStruct((m, d), BF16),
        compiler_params=_params(("arbitrary",) * 2),
        name="gated_merge",
    )(h, oa, ob, oc, w_gate, w_gate, w_gate, b_gate, b_gate, b_gate, w_branch)


def _conv_act_kernel(a_ref, g_ref, w_ref, b_ref, o_ref):
    a = a_ref[0].astype(F32)
    s = a.shape[0]
    row = lax.broadcasted_iota(jnp.int32, a.shape, 0)
    prev = jnp.where(row == 0, 0.0, pltpu.roll(a, 1, 0))
    nxt = jnp.where(row == s - 1, 0.0, pltpu.roll(a, s - 1, 0))
    w = w_ref[...]
    z = prev * w[0:1] + a * w[1:2] + nxt * w[2:3] + b_ref[...]
    gelu = 0.5 * z * (1.0 + lax.erf(z * math.sqrt(0.5)))
    o_ref[0] = (gelu * g_ref[0].astype(F32)).astype(BF16)


def _conv_act(u, conv_w, conv_b, batch, s):
    ffp = conv_w.shape[1]
    bc = _pick(ffp, 256, LANES)
    nj = ffp // bc
    u3 = u.reshape(batch, s, 2 * ffp)
    out = pl.pallas_call(
        _conv_act_kernel,
        grid=(batch, nj),
        in_specs=[pl.BlockSpec((1, s, bc), lambda b, j: (b, 0, j)),
                  pl.BlockSpec((1, s, bc), lambda b, j: (b, 0, nj + j)),
                  pl.BlockSpec((3, bc), lambda b, j: (0, j)),
                  pl.BlockSpec((1, bc), lambda b, j: (0, j))],
        out_specs=pl.BlockSpec((1, s, bc), lambda b, j: (b, 0, j)),
        out_shape=jax.ShapeDtypeStruct((batch, s, ffp), BF16),
        compiler_params=_params(("arbitrary", "arbitrary")),
        name="conv_act",
    )(u3, u3, conv_w, conv_b)
    return out.reshape(batch * s, ffp)


def _rope_tables(n):
    n_rows = n // GRID_W
    rows = jnp.broadcast_to(jnp.arange(n_rows)[:, None], (n_rows, GRID_W)).reshape(-1).astype(F32)
    cols = jnp.broadcast_to(jnp.arange(GRID_W)[None, :], (n_rows, GRID_W)).reshape(-1).astype(F32)
    quarter = D_HEAD // 4
    inv_freq = ROPE_THETA ** (-jnp.arange(quarter, dtype=F32) / quarter)
    ang = jnp.concatenate([rows[:, None] * inv_freq, cols[:, None] * inv_freq], axis=-1)
    ang = jnp.concatenate([ang, ang], axis=-1)
    sign = jnp.where(jnp.arange(D_HEAD) < D_HEAD // 2, -1.0, 1.0).astype(F32)
    return jnp.cos(ang), jnp.sin(ang) * sign


def kernel(x, c, ctx, c_ctx, w_ada, b_ada, w_in, diff_lambda, diff_subln_g, qk_norm_g, w_branch, w_gate,
           b_gate, w_o, ln1_g, ln1_b, w_up, conv_w, conv_b, w_down, ln2_g, ln2_b):
    batch, seq, d = x.shape
    n_ctx = ctx.shape[1]
    depth = w_ada.shape[0]
    ff = conv_w.shape[2]
    assert ff % LANES == 0 and w_in.shape[2] == D_IN
    ffp = -(-ff // 1024) * 1024 if ff > 1024 else ff
    alpha = (2 * depth) ** 0.25

    cvec = jnp.zeros((8, d), F32).at[:batch].set(c).at[batch].set(c_ctx)
    mod = _ada(cvec, w_ada, b_ada).reshape(depth, 8, 6, d)

    pad = ffp - ff
    conv_w_p = jnp.pad(conv_w, ((0, 0), (0, 0), (0, pad)))
    conv_b_p = jnp.pad(conv_b, ((0, 0), (0, pad))).reshape(depth, 1, ffp)
    b_gate3 = b_gate.reshape(depth, 1, N_BRANCH * d)
    w_branch2 = w_branch.reshape(depth, N_BRANCH * BRANCH_W, d)

    _, inproj_grid = _inproj_grid(batch * seq, seq)
    n_steps = inproj_grid[0] * inproj_grid[1]

    def layer_casts(l):
        casts = [_Cast(w_gate, l, n_steps), _Cast(w_branch2, l, n_steps), _Cast(w_o, l, n_steps),
                 _Cast(w_up, l, n_steps, nseg=2, cp=ffp), _Cast(w_down, l, n_steps, rp=ffp)]
        if l + 1 < depth:
            casts.append(_Cast(w_in, l + 1, n_steps))
        return casts

    w_in_b = _cast_now(_Cast(w_in, 0, 32))

    cos_l, sin_l = _rope_tables(seq)
    cos_c, sin_c = jnp.ones((n_ctx, D_HEAD), F32), jnp.zeros((n_ctx, D_HEAD), F32)
    pos_l, pos_c = _dft_tables(seq // FNET_RADIX), _dft_tables(n_ctx // FNET_RADIX)
    tw_l, tw_c = _fnet_twiddles(seq), _fnet_twiddles(n_ctx)
    chan = _dft_tables(FNET_GROUP_W)

    xl = x.reshape(batch * seq, d)
    xc = ctx.reshape(batch * n_ctx, d)
    m_ctx = batch * n_ctx
    bq_diff_l = _pick(seq, 8 * ATTN_ROWS, 8)
    bq_gqa_l = _pick(seq, 4 * ATTN_ROWS, 8)
    bq_c = _pick(n_ctx, ATTN_ROWS, 8)
    bk_ff = _pick(ffp, 2816, LANES)

    def lat(i, l):
        return mod[l, :batch, i][:, None, :]

    def cx(i, l):
        return mod[l, batch, i][None, None, :]

    hl = _modulate(xl, lat(1, 0), lat(0, 0), seq)
    hc = _modulate(xc, cx(1, 0), cx(0, 0), m_ctx)

    for l in range(depth):
        last = l == depth - 1
        lam_init = 0.8 - 0.6 * math.exp(-0.3 * l)

        pl_, w_gate_b, w_branch_b, w_o_b, w_up_b, w_down_b, *w_in_next = _inproj(
            hl, w_in_b, cos_l, sin_l, qk_norm_g[l], seq, casts=layer_casts(l))
        w_branch_b = w_branch_b.reshape(N_BRANCH, BRANCH_W, d)
        (pc_,) = _inproj(hc, w_in_b, cos_c, sin_c, qk_norm_g[l], n_ctx)
        kv = [(pl_, seq), (pc_, n_ctx)]
        oa = _diff_attention(pl_, kv, diff_lambda[l], diff_subln_g[l], lam_init, batch, seq, bq_diff_l)
        ob = _fourier_mix(pl_, batch, seq, pos_l, tw_l, chan)
        oc = _gqa_attention(pl_, kv, batch, seq, bq_gqa_l)
        merged = _gated_merge(hl, oa, ob, oc, w_gate_b, b_gate3, w_branch_b, l)
        yl = _matmul_residual(merged, w_o_b, xl, lat(2, l), seq, alpha, bk=d, name="out_proj")
        xl, hl = _ln(yl, ln1_g[l], ln1_b[l], mod=(lat(4, l), lat(3, l)), rows_per_group=seq)

        if not last:
            kvc = [(pc_, n_ctx)]
            oa_c = _diff_attention(pc_, kvc, diff_lambda[l], diff_subln_g[l], lam_init, batch, n_ctx, bq_c)
            ob_c = _fourier_mix(pc_, batch, n_ctx, pos_c, tw_c, chan)
            oc_c = _gqa_attention(pc_, kvc, batch, n_ctx, bq_c)
            merged_c = _gated_merge(hc, oa_c, ob_c, oc_c, w_gate_b, b_gate3, w_branch_b, l)
            yc = _matmul_residual(merged_c, w_o_b, xc, cx(2, l), m_ctx, alpha, bk=d, name="out_proj_ctx")
            xc, hc = _ln(yc, ln1_g[l], ln1_b[l], mod=(cx(4, l), cx(3, l)), rows_per_group=m_ctx)

        def ffn(h, xres, gate, rows_per_group, s):
            mrows = h.shape[0]
            u = _matmul(h, w_up_b, bm=_pick(rows_per_group, 1024, 8), bn=_pick(2 * ffp, 1024, LANES), bk=d,
                        out_dtype=BF16, epilogue=_identity, name="ffn_up")
            act = _conv_act(u, conv_w_p[l], conv_b_p[l], mrows // s, s)
            return _matmul_residual(act, w_down_b, xres, gate, rows_per_group, alpha, bk=bk_ff,
                                    name="ffn_down")

        yl = ffn(hl, xl, lat(5, l), seq, seq)
        if last:
            xl = _ln(yl, ln2_g[l], ln2_b[l])
        else:
            xl, hl = _ln(yl, ln2_g[l], ln2_b[l], mod=(lat(1, l + 1), lat(0, l + 1)), rows_per_group=seq)
            yc = ffn(hc, xc, cx(5, l), m_ctx, n_ctx)
            xc, hc = _ln(yc, ln2_g[l], ln2_b[l], mod=(cx(1, l + 1), cx(0, l + 1)), rows_per_group=m_ctx)

        if not last:
            (w_in_b,) = w_in_next

    return xl.reshape(batch, seq, d)
```

```python
import functools
import math

import jax
import jax.numpy as jnp
from jax import lax
from jax.experimental import pallas as pl
from jax.experimental.pallas import tpu as pltpu

F32 = jnp.float32
BF16 = jnp.bfloat16

GRID_W = 64
D_HEAD = 128
H_DIFF = 8
DIFF_W = 2 * D_HEAD
FNET_GROUPS = 8
FNET_GROUP_W = 256
H_GQA = 16
H_KV = 4
GQA_GROUP = H_GQA // H_KV
GQA_W = GQA_GROUP * D_HEAD
BRANCH_W = 2048
N_BRANCH = 3
ROPE_THETA = 10000.0
NORM_EPS = 1e-6

OFF_AQ = 0
OFF_AK = OFF_AQ + H_DIFF * DIFF_W
OFF_AV = OFF_AK + H_DIFF * DIFF_W
OFF_B = OFF_AV + H_DIFF * DIFF_W
OFF_CQ = OFF_B + FNET_GROUPS * FNET_GROUP_W
OFF_CK = OFF_CQ + H_GQA * D_HEAD
OFF_CV = OFF_CK + H_KV * D_HEAD
D_IN = OFF_CV + H_KV * D_HEAD
Q_SCALE = D_HEAD ** -0.5 * math.log2(math.e)

V7X_VMEM_LIMIT_BYTES = 56 * 1024 * 1024
LANES = 128
IN_TILE = 512
ATTN_ROWS = 256


def _params(sem):
    return pltpu.CompilerParams(dimension_semantics=sem, vmem_limit_bytes=V7X_VMEM_LIMIT_BYTES)


def _pick(dim, pref, align):
    t = (min(pref, dim) // align) * align
    while t >= align:
        if dim % t == 0:
            return t
        t -= align
    return dim


def _dot(a, b):
    return jnp.dot(a, b, preferred_element_type=F32)


def _dot_nt(a, b):
    return lax.dot_general(a, b, (((1,), (1,)), ((), ())), preferred_element_type=F32)


def _pipelined(n_chains, issue, finish, lookahead=1):
    pending = [issue(c) for c in range(min(lookahead, n_chains))]
    for c in range(n_chains):
        if c + lookahead < n_chains:
            pending.append(issue(c + lookahead))
        finish(c, pending.pop(0))


class _Cast:
    def __init__(self, w, layer, max_blocks, *, nseg=1, cp=None, rp=None):
        _, r, c_all = w.shape
        self.w, self.layer, self.nseg = w, layer, nseg
        self.c = c_all // nseg
        self.cp = cp or self.c
        rows_out = rp or r
        div = math.gcd(r, rows_out)
        self.rb = next(rb for rb in range(16, div + 1, 16) if div % rb == 0 and rows_out // rb <= max_blocks)
        self.n_in, self.n_out = r // self.rb, rows_out // self.rb
        self.out_shape = jax.ShapeDtypeStruct((1, rows_out, nseg * self.cp), BF16)

    def specs(self, step):
        out_blk = lambda *g: jnp.minimum(step(*g), self.n_out - 1)
        return (pl.BlockSpec((1, self.rb, self.nseg * self.c),
                             lambda *g: (self.layer, jnp.minimum(out_blk(*g), self.n_in - 1), 0)),
                pl.BlockSpec((1, self.rb, self.nseg * self.cp), lambda *g: (0, out_blk(*g), 0)))

    def body(self, t, x_ref, o_ref):
        x = x_ref[0]
        if self.n_out > self.n_in:
            x = jnp.where(jnp.minimum(t, self.n_out - 1) < self.n_in, x, 0.0)
        for s in range(self.nseg):
            o_ref[0, :, s * self.cp:s * self.cp + self.c] = x[:, s * self.c:(s + 1) * self.c].astype(BF16)
            if self.cp > self.c:
                o_ref[0, :, s * self.cp + self.c:(s + 1) * self.cp] = jnp.zeros((self.rb, self.cp - self.c), BF16)


def _run_casts(t, casts, refs):
    for n, cast in enumerate(casts):
        cast.body(t, refs[n], refs[len(casts) + n])


def _cast_kernel(x_ref, o_ref, *, cast):
    cast.body(pl.program_id(0), x_ref, o_ref)


def _cast_now(cast):
    in_spec, out_spec = cast.specs(lambda t: t)
    return pl.pallas_call(
        functools.partial(_cast_kernel, cast=cast), grid=(cast.n_out,), in_specs=[in_spec],
        out_specs=out_spec, out_shape=cast.out_shape,
        compiler_params=_params(("arbitrary",)), name="cast_weight",
    )(cast.w)


def _bf16_pieces(x, n):
    pieces = []
    for _ in range(n):
        p = x.astype(BF16).astype(F32)
        pieces.append(p)
        x = x - p
    return pieces


def _ada_kernel(c_ref, w_ref, b_ref, o_ref):
    c = c_ref[...]
    rows = c.shape[0]
    s3 = jnp.concatenate(_bf16_pieces(c * jax.nn.sigmoid(c), 3), axis=0).astype(BF16)
    w_hi, w_lo = _bf16_pieces(w_ref[0], 2)
    hi = _dot(s3, w_hi.astype(BF16))
    lo = _dot(s3, w_lo.astype(BF16))
    acc = (hi[:rows] + hi[rows:2 * rows] + hi[2 * rows:]) + (lo[:rows] + lo[rows:2 * rows])
    o_ref[0] = acc + b_ref[0]


def _ada(cvec, w_ada, b_ada):
    depth, d, n = w_ada.shape
    bn = _pick(n, 512, LANES)
    return pl.pallas_call(
        _ada_kernel,
        grid=(depth, n // bn),
        in_specs=[pl.BlockSpec((8, d), lambda l, j: (0, 0)),
                  pl.BlockSpec((1, d, bn), lambda l, j: (l, 0, j)),
                  pl.BlockSpec((1, 1, bn), lambda l, j: (l, 0, j))],
        out_specs=pl.BlockSpec((1, 8, bn), lambda l, j: (l, 0, j)),
        out_shape=jax.ShapeDtypeStruct((depth, 8, n), F32),
        compiler_params=_params(("arbitrary", "arbitrary")),
        name="ada",
    )(cvec, w_ada, b_ada.reshape(depth, 1, n))


def _mod_kernel(x_ref, sc_ref, sh_ref, h_ref):
    h_ref[...] = (x_ref[...] * (1.0 + sc_ref[0]) + sh_ref[0]).astype(BF16)


def _modulate(x, sc, sh, rows_per_group):
    m, d = x.shape
    bs = _pick(rows_per_group, 256, 8)
    grp = lambda i: (i * bs // rows_per_group, 0, 0)
    return pl.pallas_call(
        _mod_kernel,
        grid=(m // bs,),
        in_specs=[pl.BlockSpec((bs, d), lambda i: (i, 0)),
                  pl.BlockSpec((1, 1, d), grp),
                  pl.BlockSpec((1, 1, d), grp)],
        out_specs=pl.BlockSpec((bs, d), lambda i: (i, 0)),
        out_shape=jax.ShapeDtypeStruct((m, d), BF16),
        compiler_params=_params(("arbitrary",)),
        name="modulate",
    )(x, sc, sh)


def _layer_norm(y, g, b):
    mu = jnp.mean(y, axis=-1, keepdims=True)
    dlt = y - mu
    var = jnp.mean(dlt * dlt, axis=-1, keepdims=True)
    return dlt * lax.rsqrt(var + NORM_EPS) * g + b


def _ln_mod_kernel(y_ref, g_ref, b_ref, sc_ref, sh_ref, x_ref, h_ref):
    xn = _layer_norm(y_ref[...], g_ref[...], b_ref[...])
    x_ref[...] = xn
    h_ref[...] = (xn * (1.0 + sc_ref[0]) + sh_ref[0]).astype(BF16)


def _ln_kernel(y_ref, g_ref, b_ref, x_ref):
    x_ref[...] = _layer_norm(y_ref[...], g_ref[...], b_ref[...])


def _ln(y, g, b, mod=None, rows_per_group=None):
    m, d = y.shape
    bs = _pick(rows_per_group or m, 256, 8)
    row = pl.BlockSpec((bs, d), lambda i: (i, 0))
    vec = pl.BlockSpec((1, d), lambda i: (0, 0))
    g2, b2 = g.reshape(1, d), b.reshape(1, d)
    if mod is None:
        return pl.pallas_call(
            _ln_kernel, grid=(m // bs,), in_specs=[row, vec, vec], out_specs=row,
            out_shape=jax.ShapeDtypeStruct((m, d), F32),
            compiler_params=_params(("arbitrary",)), name="layernorm",
        )(y, g2, b2)
    grp = pl.BlockSpec((1, 1, d), lambda i: (i * bs // rows_per_group, 0, 0))
    return pl.pallas_call(
        _ln_mod_kernel, grid=(m // bs,), in_specs=[row, vec, vec, grp, grp],
        out_specs=[row, row],
        out_shape=[jax.ShapeDtypeStruct((m, d), F32), jax.ShapeDtypeStruct((m, d), BF16)],
        compiler_params=_params(("arbitrary",)), name="layernorm_mod",
    )(y, g2, b2, mod[0], mod[1])


def _mm_kernel(*refs, nk, n_extra, epilogue):
    x_ref, w_ref = refs[0], refs[1]
    extra = refs[2:2 + n_extra]
    o_ref = refs[2 + n_extra]
    part = _dot(x_ref[...], w_ref[0])
    if nk == 1:
        o_ref[...] = epilogue(part, *extra).astype(o_ref.dtype)
        return
    acc_ref = refs[3 + n_extra]
    k = pl.program_id(2)

    @pl.when(k == 0)
    def _():
        acc_ref[...] = part

    @pl.when((k > 0) & (k < nk - 1))
    def _():
        acc_ref[...] += part

    @pl.when(k == nk - 1)
    def _():
        o_ref[...] = epilogue(acc_ref[...] + part, *extra).astype(o_ref.dtype)


def _matmul(x, w, *, bm, bn, bk, out_dtype, epilogue, extra=(), extra_specs=(), name):
    m, kdim = x.shape
    n = w.shape[2]
    nk = kdim // bk
    kernel = functools.partial(_mm_kernel, nk=nk, n_extra=len(extra), epilogue=epilogue)
    return pl.pallas_call(
        kernel,
        grid=(m // bm, n // bn, nk),
        in_specs=[pl.BlockSpec((bm, bk), lambda i, j, k: (i, k)),
                  pl.BlockSpec((1, bk, bn), lambda i, j, k: (0, k, j)),
                  *extra_specs],
        out_specs=pl.BlockSpec((bm, bn), lambda i, j, k: (i, j)),
        out_shape=jax.ShapeDtypeStruct((m, n), out_dtype),
        scratch_shapes=[pltpu.VMEM((bm, bn), F32)] if nk > 1 else [],
        compiler_params=_params(("arbitrary", "arbitrary", "arbitrary")),
        name=name,
    )(x, w, *extra)


def _identity(acc):
    return acc


def _residual_epilogue(acc, x_ref, gate_ref, *, alpha):
    return alpha * x_ref[...] + gate_ref[0] * acc


def _matmul_residual(a, w, xres, gate, rows_per_group, alpha, *, bk, name):
    n = w.shape[2]
    bm = _pick(rows_per_group, 1024, 8)
    bn = _pick(n, 1024, LANES)
    return _matmul(
        a, w, bm=bm, bn=bn, bk=bk, out_dtype=F32,
        epilogue=functools.partial(_residual_epilogue, alpha=alpha),
        extra=(xres, gate),
        extra_specs=(pl.BlockSpec((bm, bn), lambda i, j, k: (i, j)),
                     pl.BlockSpec((1, 1, bn), lambda i, j, k: (i * bm // rows_per_group, 0, j))),
        name=name)


def _rope(x, cos, sin_signed):
    return x * cos + pltpu.roll(x, D_HEAD // 2, 1) * sin_signed


def _inproj_kernel(h_ref, w_ref, cos_ref, sin_ref, g_ref, *refs, casts, n_col_tiles):
    o_ref = refs[len(casts)]
    i, j = pl.program_id(0), pl.program_id(1)
    _run_casts(i * n_col_tiles + j, casts, refs[:len(casts)] + refs[len(casts) + 1:])
    acc = _dot(h_ref[...], w_ref[0])
    n_chunks = IN_TILE // D_HEAD
    t_ak, t_av, t_cq, t_ck, t_cv = (OFF_AK // IN_TILE, OFF_AV // IN_TILE, OFF_CQ // IN_TILE,
                                    OFF_CK // IN_TILE, OFF_CV // IN_TILE)

    @pl.when(j < t_av)
    def _():
        s = jnp.where(j < t_ak, Q_SCALE, 1.0).astype(F32)
        cos = cos_ref[...] * s
        sin = sin_ref[...] * s
        for c in range(n_chunks):
            sl = slice(c * D_HEAD, (c + 1) * D_HEAD)
            o_ref[:, sl] = _rope(acc[:, sl], cos, sin).astype(BF16)

    @pl.when(((j >= t_av) & (j < t_cq)) | (j >= t_cv))
    def _():
        o_ref[...] = acc.astype(BF16)

    @pl.when((j >= t_cq) & (j < t_cv))
    def _():
        is_q = j < t_ck
        g = jnp.where(is_q, g_ref[0:1, :], g_ref[1:2, :])
        s = jnp.where(is_q, Q_SCALE, 1.0).astype(F32)
        cos = cos_ref[...] * s
        sin = sin_ref[...] * s
        for c in range(n_chunks):
            sl = slice(c * D_HEAD, (c + 1) * D_HEAD)
            x = acc[:, sl]
            ms = jnp.mean(x * x, axis=-1, keepdims=True)
            xn = x * lax.rsqrt(ms + NORM_EPS) * g
            o_ref[:, sl] = _rope(xn, cos, sin).astype(BF16)


def _inproj_grid(m, rows_per_seq):
    bm = _pick(rows_per_seq, 1024, 8)
    return bm, (m // bm, D_IN // IN_TILE)


def _inproj(h, w, cos, sin_signed, qk_g, rows_per_seq, casts=()):
    m, d = h.shape
    bm, grid = _inproj_grid(m, rows_per_seq)
    nseq_blk = rows_per_seq // bm
    tab = pl.BlockSpec((bm, D_HEAD), lambda i, j: (i % nseq_blk, 0))
    cast_specs = [cast.specs(lambda i, j: i * grid[1] + j) for cast in casts]
    return pl.pallas_call(
        functools.partial(_inproj_kernel, casts=tuple(casts), n_col_tiles=grid[1]),
        grid=grid,
        in_specs=[pl.BlockSpec((bm, d), lambda i, j: (i, 0)),
                  pl.BlockSpec((1, d, IN_TILE), lambda i, j: (0, 0, j)),
                  tab, tab,
                  pl.BlockSpec((2, D_HEAD), lambda i, j: (0, 0)),
                  *[sp[0] for sp in cast_specs]],
        out_specs=[pl.BlockSpec((bm, IN_TILE), lambda i, j: (i, j)), *[sp[1] for sp in cast_specs]],
        out_shape=[jax.ShapeDtypeStruct((m, D_IN), BF16), *[cast.out_shape for cast in casts]],
        compiler_params=_params(("arbitrary", "arbitrary")),
        name="inproj",
    )(h, w, cos, sin_signed, qk_g, *[cast.w for cast in casts])


KEY_CHUNK = 1024


class _SoftmaxChain:
    def __init__(self, q, k_refs, k_lo, values, ones_at=None):
        self.q, self.k_refs, self.k_lo, self.values, self.ones_at = q, k_refs, k_lo, values, ones_at
        self.chunks = [(n, c0, min(c0 + KEY_CHUNK, k_ref.shape[0]))
                       for n, k_ref in enumerate(k_refs) for c0 in range(0, k_ref.shape[0], KEY_CHUNK)]
        self.scores = []
        self.lane_max = self.lane_sum = self.acc = self.row_max = None

    def score(self, i):
        n, c0, c1 = self.chunks[i]
        s = _dot_nt(self.q, self.k_refs[n][c0:c1, self.k_lo:self.k_lo + D_HEAD])
        self.scores.append(s)
        for t in range(0, c1 - c0, LANES):
            tile = s[:, t:t + LANES]
            self.lane_max = tile if self.lane_max is None else jnp.maximum(self.lane_max, tile)

    def accumulate(self, i):
        if self.row_max is None:
            self.row_max = jnp.max(self.lane_max, axis=-1, keepdims=True)
        n, c0, c1 = self.chunks[i]
        x = self.scores[i] - self.row_max
        if self.ones_at is None:
            e = jnp.exp2(x)
            for t in range(0, c1 - c0, LANES):
                tile = e[:, t:t + LANES]
                self.lane_sum = tile if self.lane_sum is None else self.lane_sum + tile
            e = e.astype(BF16)
        else:
            e = jnp.exp2(x.astype(BF16))
        pv = _dot(e, self.values[n][c0:c1])
        self.acc = pv if self.acc is None else self.acc + pv

    def result(self):
        if self.ones_at is None:
            return self.acc, jnp.sum(self.lane_sum, axis=-1, keepdims=True)
        return self.acc[:, :self.ones_at], self.acc[:, self.ones_at:self.ones_at + 1]


def _run_chains(n_blocks, make_chains, write):
    cur = make_chains(0)
    for i in range(len(cur[0].chunks)):
        for ch in cur:
            ch.score(i)
    for b in range(n_blocks):
        nxt = make_chains(b + 1) if b + 1 < n_blocks else []
        for i in range(len(cur[0].chunks)):
            for ch in nxt:
                ch.score(i)
            for ch in cur:
                ch.accumulate(i)
        write(b, cur)
        cur = nxt


def _diff_attn_kernel(*refs, n_src, lam_init):
    q_ref = refs[0]
    k_refs = refs[1:1 + n_src]
    v_refs = refs[1 + n_src:1 + 2 * n_src]
    lam_ref, g_ref, o_ref = refs[1 + 2 * n_src:]
    lp = lam_ref[...]
    lam = (jnp.exp(jnp.sum(lp[0:1] * lp[1:2], axis=-1, keepdims=True))
           - jnp.exp(jnp.sum(lp[2:3] * lp[3:4], axis=-1, keepdims=True)) + lam_init)
    rows = min(ATTN_ROWS, q_ref.shape[0])

    def make_chains(b):
        q = q_ref[b * rows:(b + 1) * rows, :]
        return [_SoftmaxChain(q[:, lo:lo + D_HEAD], k_refs, lo, v_refs) for lo in (0, D_HEAD)]

    def write(b, chains):
        (o1, den1), (o2, den2) = chains[0].result(), chains[1].result()
        o = o1 * (1.0 / den1) - o2 * (lam / den2)
        ms = jnp.mean(o * o, axis=-1, keepdims=True)
        y = (o * lax.rsqrt(ms + NORM_EPS)) * g_ref[...] * (1.0 - lam_init)
        o_ref[b * rows:(b + 1) * rows, :] = y.astype(BF16)

    _run_chains(q_ref.shape[0] // rows, make_chains, write)


def _diff_attention(pq, kv_sources, lam_p, subln_g, lam_init, batch, q_len, bq):
    nq = q_len // bq
    kblk, vblk = OFF_AK // DIFF_W, OFF_AV // DIFF_W
    k_specs = [pl.BlockSpec((n, DIFF_W), lambda b, h, i: (b, kblk + h)) for _, n in kv_sources]
    v_specs = [pl.BlockSpec((n, DIFF_W), lambda b, h, i: (b, vblk + h)) for _, n in kv_sources]
    srcs = [p for p, _ in kv_sources]
    kernel = functools.partial(_diff_attn_kernel, n_src=len(srcs), lam_init=lam_init)
    return pl.pallas_call(
        kernel,
        grid=(batch, H_DIFF, nq),
        in_specs=[pl.BlockSpec((bq, DIFF_W), lambda b, h, i: (b * nq + i, h)),
                  *k_specs, *v_specs,
                  pl.BlockSpec((4, D_HEAD), lambda b, h, i: (0, 0)),
                  pl.BlockSpec((1, DIFF_W), lambda b, h, i: (0, 0))],
        out_specs=pl.BlockSpec((bq, DIFF_W), lambda b, h, i: (b * nq + i, h)),
        out_shape=jax.ShapeDtypeStruct((batch * q_len, H_DIFF * DIFF_W), BF16),
        compiler_params=_params(("arbitrary", "arbitrary", "arbitrary")),
        name="diff_attention",
    )(pq, *srcs, *srcs, lam_p, subln_g.reshape(1, DIFF_W))


def _gqa_kernel(*refs, n_src):
    q_ref = refs[0]
    k_refs = refs[1:1 + n_src]
    v_refs = refs[1 + n_src:1 + 2 * n_src]
    o_ref = refs[1 + 2 * n_src]
    rows = min(ATTN_ROWS // 2, q_ref.shape[0])
    heads = [slice(g * D_HEAD, (g + 1) * D_HEAD) for g in range(GQA_GROUP)]
    values = [jnp.concatenate([v_ref[...], jnp.ones(v_ref.shape, BF16)], axis=1) for v_ref in v_refs]

    def make_chains(b):
        rs = slice(b * rows, (b + 1) * rows)
        q = jnp.concatenate([q_ref[rs, cs] for cs in heads], axis=0)
        return [_SoftmaxChain(q, k_refs, 0, values, ones_at=D_HEAD)]

    def write(b, chains):
        o, den = chains[0].result()
        o = (o / den).astype(BF16)
        for g, cs in enumerate(heads):
            o_ref[b * rows:(b + 1) * rows, cs] = o[g * rows:(g + 1) * rows]

    _run_chains(q_ref.shape[0] // rows, make_chains, write)


def _gqa_attention(pq, kv_sources, batch, q_len, bq):
    nq = q_len // bq
    qblk, kblk, vblk = OFF_CQ // GQA_W, OFF_CK // D_HEAD, OFF_CV // D_HEAD
    k_specs = [pl.BlockSpec((n, D_HEAD), lambda b, kh, i: (b, kblk + kh)) for _, n in kv_sources]
    v_specs = [pl.BlockSpec((n, D_HEAD), lambda b, kh, i: (b, vblk + kh)) for _, n in kv_sources]
    srcs = [p for p, _ in kv_sources]
    return pl.pallas_call(
        functools.partial(_gqa_kernel, n_src=len(srcs)),
        grid=(batch, H_KV, nq),
        in_specs=[pl.BlockSpec((bq, GQA_W), lambda b, kh, i: (b * nq + i, qblk + kh)),
                  *k_specs, *v_specs],
        out_specs=pl.BlockSpec((bq, GQA_W), lambda b, kh, i: (b * nq + i, kh)),
        out_shape=jax.ShapeDtypeStruct((batch * q_len, H_GQA * D_HEAD), BF16),
        compiler_params=_params(("arbitrary",) * 3),
        name="gqa_attention",
    )(pq, *srcs, *srcs)


def _dft_tables(n, split=64):
    k = jnp.arange(n, dtype=jnp.int32)

    def cos_sin(t):
        ang = ((k[:, None] * t[None, :]) % n).astype(F32) * (2.0 * math.pi / n)
        return jnp.cos(ang), jnp.sin(ang)

    if n <= split or n % split:
        c, s = cos_sin(k)
    else:
        ca, sa = (v[:, :, None] for v in cos_sin(jnp.arange(0, n, split, dtype=jnp.int32)))
        cb, sb = (v[:, None, :] for v in cos_sin(jnp.arange(split, dtype=jnp.int32)))
        c = (ca * cb - sa * sb).reshape(n, n)
        s = (sa * cb + ca * sb).reshape(n, n)
    return c.astype(BF16), s.astype(BF16)


FNET_RADIX = 4


def _fnet_twiddles(n):
    p = jnp.arange(1, FNET_RADIX, dtype=jnp.int32)[:, None]
    k = jnp.arange(n // FNET_RADIX, dtype=jnp.int32)[None, :]
    ang = ((p * k) % n).astype(F32) * (2.0 * math.pi / n)
    return jnp.cos(ang)[:, :, None], jnp.sin(ang)[:, :, None]


def _fnet_kernel(c_ref, s_ref, x0_ref, x1_ref, x2_ref, x3_ref, twc_ref, tws_ref, cc_ref, sc_ref, o_ref, *, scale):
    c, s = c_ref[...], s_ref[...]
    ys = []
    for p, x_ref in enumerate((x0_ref, x1_ref, x2_ref, x3_ref)):
        x = x_ref[...]
        zr, zi = _dot(c, x), -_dot(s, x)
        if p:
            tc, ts = twc_ref[p - 1], tws_ref[p - 1]
            zr, zi = zr * tc + zi * ts, zi * tc - zr * ts
        ys.append((zr, zi))
    (ar, ai), (br, bi), (cr, ci), (dr, di) = ys
    zs = ((ar + br + cr + dr, ai + bi + ci + di),
          (ar + bi - cr - di, ai - br - ci + dr),
          (ar - br + cr - dr, ai - bi + ci - di),
          (ar - bi - cr + di, ai + br - ci - dr))
    for q, (zr, zi) in enumerate(zs):
        zr, zi = zr.astype(BF16), zi.astype(BF16)
        for g in range(zr.shape[1] // FNET_GROUP_W):
            sl = slice(g * FNET_GROUP_W, (g + 1) * FNET_GROUP_W)
            y = _dot(zr[:, sl], cc_ref[...]) + _dot(zi[:, sl], sc_ref[...])
            o_ref[0, q, :, sl] = (y * scale).astype(BF16)


def _fourier_mix(p, batch, n, pos_tabs, twiddles, chan_tabs):
    nr = n // FNET_RADIX
    bk = _pick(nr, 512, 16)
    bc = 2 * FNET_GROUP_W
    width = FNET_GROUPS * FNET_GROUP_W
    pos = pl.BlockSpec((bk, nr), lambda b, i, j: (i, 0))
    tw = pl.BlockSpec((FNET_RADIX - 1, bk, 1), lambda b, i, j: (0, i, 0))
    chan = pl.BlockSpec((FNET_GROUP_W, FNET_GROUP_W), lambda b, i, j: (0, 0))
    p4 = p[:, OFF_B:OFF_B + width].reshape(batch * nr, FNET_RADIX * width)
    xs = [pl.BlockSpec((nr, bc), lambda b, i, j, r=r: (b, r * width // bc + j)) for r in range(FNET_RADIX)]
    scale = 1.0 / math.sqrt(n * FNET_GROUP_W)
    out = pl.pallas_call(
        functools.partial(_fnet_kernel, scale=scale),
        grid=(batch, nr // bk, width // bc),
        in_specs=[pos, pos, *xs, tw, tw, chan, chan],
        out_specs=pl.BlockSpec((1, FNET_RADIX, bk, bc), lambda b, i, j: (b, 0, i, j)),
        out_shape=jax.ShapeDtypeStruct((batch, FNET_RADIX, nr, width), BF16),
        compiler_params=_params(("arbitrary",) * 3),
        name="fourier_mix",
    )(*pos_tabs, p4, p4, p4, p4, *twiddles, *chan_tabs)
    return out.reshape(batch * n, width)


def _merge_kernel(h_ref, oa_ref, ob_ref, oc_ref, wg0_ref, wg1_ref, wg2_ref, bg0_ref, bg1_ref, bg2_ref,
                  wb_ref, o_ref):
    h = h_ref[...]
    acc = None
    for n, (br_ref, wg_ref, bg_ref) in enumerate(((oa_ref, wg0_ref, bg0_ref), (ob_ref, wg1_ref, bg1_ref),
                                                  (oc_ref, wg2_ref, bg2_ref))):
        gate = jax.nn.sigmoid(_dot(h, wg_ref[0]) + bg_ref[0])
        term = gate * _dot(br_ref[...], wb_ref[n])
        acc = term if acc is None else acc + term
    o_ref[...] = acc.astype(BF16)


def _gated_merge(h, oa, ob, oc, w_gate, b_gate, w_branch, layer):
    m, d = h.shape
    bm = _pick(m, 1024, 8)
    bn = _pick(d, 256, LANES)
    nj = d // bn
    once = pl.Buffered(1)
    br = pl.BlockSpec((bm, BRANCH_W), lambda i, j: (i, 0), pipeline_mode=once)
    wg = [pl.BlockSpec((1, d, bn), lambda i, j, n=n: (0, 0, n * nj + j)) for n in range(N_BRANCH)]
    bg = [pl.BlockSpec((1, 1, bn), lambda i, j, n=n: (layer, 0, n * nj + j)) for n in range(N_BRANCH)]
    return pl.pallas_call(
        _merge_kernel,
        grid=(m // bm, nj),
        in_specs=[pl.BlockSpec((bm, d), lambda i, j: (i, 0), pipeline_mode=once), br, br, br, *wg, *bg,
                  pl.BlockSpec((N_BRANCH, BRANCH_W, bn), lambda i, j: (0, 0, j))],
        out_specs=pl.BlockSpec((bm, bn), lambda i, j: (i, j)),
        out_shape=jax.ShapeDtypeStruct((m, d), BF16),
        compiler_params=_params(("arbitrary",) * 2),
        name="gated_merge",
    )(h, oa, ob, oc, w_gate, w_gate, w_gate, b_gate, b_gate, b_gate, w_branch)


def _conv_act_kernel(a_ref, g_ref, w_ref, b_ref, o_ref):
    a = a_ref[0].astype(F32)
    s = a.shape[0]
    row = lax.broadcasted_iota(jnp.int32, a.shape, 0)
    prev = jnp.where(row == 0, 0.0, pltpu.roll(a, 1, 0))
    nxt = jnp.where(row == s - 1, 0.0, pltpu.roll(a, s - 1, 0))
    w = w_ref[...]
    z = prev * w[0:1] + a * w[1:2] + nxt * w[2:3] + b_ref[...]
    gelu = 0.5 * z * (1.0 + lax.erf(z * math.sqrt(0.5)))
    o_ref[0] = (gelu * g_ref[0].astype(F32)).astype(BF16)


def _conv_act(u, conv_w, conv_b, batch, s):
    ffp = conv_w.shape[1]
    bc = _pick(ffp, 256, LANES)
    nj = ffp // bc
    u3 = u.reshape(batch, s, 2 * ffp)
    out = pl.pallas_call(
        _conv_act_kernel,
        grid=(batch, nj),
        in_specs=[pl.BlockSpec((1, s, bc), lambda b, j: (b, 0, j)),
                  pl.BlockSpec((1, s, bc), lambda b, j: (b, 0, nj + j)),
                  pl.BlockSpec((3, bc), lambda b, j: (0, j)),
                  pl.BlockSpec((1, bc), lambda b, j: (0, j))],
        out_specs=pl.BlockSpec((1, s, bc), lambda b, j: (b, 0, j)),
        out_shape=jax.ShapeDtypeStruct((batch, s, ffp), BF16),
        compiler_params=_params(("arbitrary", "arbitrary")),
        name="conv_act",
    )(u3, u3, conv_w, conv_b)
    return out.reshape(batch * s, ffp)


def _rope_tables(n):
    n_rows = n // GRID_W
    rows = jnp.broadcast_to(jnp.arange(n_rows)[:, None], (n_rows, GRID_W)).reshape(-1).astype(F32)
    cols = jnp.broadcast_to(jnp.arange(GRID_W)[None, :], (n_rows, GRID_W)).reshape(-1).astype(F32)
    quarter = D_HEAD // 4
    inv_freq = ROPE_THETA ** (-jnp.arange(quarter, dtype=F32) / quarter)
    ang = jnp.concatenate([rows[:, None] * inv_freq, cols[:, None] * inv_freq], axis=-1)
    ang = jnp.concatenate([ang, ang], axis=-1)
    sign = jnp.where(jnp.arange(D_HEAD) < D_HEAD // 2, -1.0, 1.0).astype(F32)
    return jnp.cos(ang), jnp.sin(ang) * sign


def kernel(x, c, ctx, c_ctx, w_ada, b_ada, w_in, diff_lambda, diff_subln_g, qk_norm_g, w_branch, w_gate,
           b_gate, w_o, ln1_g, ln1_b, w_up, conv_w, conv_b, w_down, ln2_g, ln2_b):
    batch, seq, d = x.shape
    n_ctx = ctx.shape[1]
    depth = w_ada.shape[0]
    ff = conv_w.shape[2]
    assert ff % LANES == 0 and w_in.shape[2] == D_IN
    ffp = -(-ff // 1024) * 1024 if ff > 1024 else ff
    alpha = (2 * depth) ** 0.25

    cvec = jnp.zeros((8, d), F32).at[:batch].set(c).at[batch].set(c_ctx)
    mod = _ada(cvec, w_ada, b_ada).reshape(depth, 8, 6, d)

    pad = ffp - ff
    conv_w_p = jnp.pad(conv_w, ((0, 0), (0, 0), (0, pad)))
    conv_b_p = jnp.pad(conv_b, ((0, 0), (0, pad))).reshape(depth, 1, ffp)
    b_gate3 = b_gate.reshape(depth, 1, N_BRANCH * d)
    w_branch2 = w_branch.reshape(depth, N_BRANCH * BRANCH_W, d)

    _, inproj_grid = _inproj_grid(batch * seq, seq)
    n_steps = inproj_grid[0] * inproj_grid[1]

    def layer_casts(l):
        casts = [_Cast(w_gate, l, n_steps), _Cast(w_branch2, l, n_steps), _Cast(w_o, l, n_steps),
                 _Cast(w_up, l, n_steps, nseg=2, cp=ffp), _Cast(w_down, l, n_steps, rp=ffp)]
        if l + 1 < depth:
            casts.append(_Cast(w_in, l + 1, n_steps))
        return casts

    w_in_b = _cast_now(_Cast(w_in, 0, 32))

    cos_l, sin_l = _rope_tables(seq)
    cos_c, sin_c = jnp.ones((n_ctx, D_HEAD), F32), jnp.zeros((n_ctx, D_HEAD), F32)
    pos_l, pos_c = _dft_tables(seq // FNET_RADIX), _dft_tables(n_ctx // FNET_RADIX)
    tw_l, tw_c = _fnet_twiddles(seq), _fnet_twiddles(n_ctx)
    chan = _dft_tables(FNET_GROUP_W)

    xl = x.reshape(batch * seq, d)
    xc = ctx.reshape(batch * n_ctx, d)
    m_ctx = batch * n_ctx
    bq_diff_l = _pick(seq, 8 * ATTN_ROWS, 8)
    bq_gqa_l = _pick(seq, 4 * ATTN_ROWS, 8)
    bq_c = _pick(n_ctx, ATTN_ROWS, 8)
    bk_ff = _pick(ffp, 2816, LANES)

    def lat(i, l):
        return mod[l, :batch, i][:, None, :]

    def cx(i, l):
        return mod[l, batch, i][None, None, :]

    hl = _modulate(xl, lat(1, 0), lat(0, 0), seq)
    hc = _modulate(xc, cx(1, 0), cx(0, 0), m_ctx)

    for l in range(depth):
        last = l == depth - 1
        lam_init = 0.8 - 0.6 * math.exp(-0.3 * l)

        pl_, w_gate_b, w_branch_b, w_o_b, w_up_b, w_down_b, *w_in_next = _inproj(
            hl, w_in_b, cos_l, sin_l, qk_norm_g[l], seq, casts=layer_casts(l))
        w_branch_b = w_branch_b.reshape(N_BRANCH, BRANCH_W, d)
        (pc_,) = _inproj(hc, w_in_b, cos_c, sin_c, qk_norm_g[l], n_ctx)
        kv = [(pl_, seq), (pc_, n_ctx)]
        oa = _diff_attention(pl_, kv, diff_lambda[l], diff_subln_g[l], lam_init, batch, seq, bq_diff_l)
        ob = _fourier_mix(pl_, batch, seq, pos_l, tw_l, chan)
        oc = _gqa_attention(pl_, kv, batch, seq, bq_gqa_l)
        merged = _gated_merge(hl, oa, ob, oc, w_gate_b, b_gate3, w_branch_b, l)
        yl = _matmul_residual(merged, w_o_b, xl, lat(2, l), seq, alpha, bk=d, name="out_proj")
        xl, hl = _ln(yl, ln1_g[l], ln1_b[l], mod=(lat(4, l), lat(3, l)), rows_per_group=seq)

        if not last:
            kvc = [(pc_, n_ctx)]
            oa_c = _diff_attention(pc_, kvc, diff_lambda[l], diff_subln_g[l], lam_init, batch, n_ctx, bq_c)
            ob_c = _fourier_mix(pc_, batch, n_ctx, pos_c, tw_c, chan)
            oc_c = _gqa_attention(pc_, kvc, batch, n_ctx, bq_c)
            merged_c = _gated_merge(hc, oa_c, ob_c, oc_c, w_gate_b, b_gate3, w_branch_b, l)
            yc = _matmul_residual(merged_c, w_o_b, xc, cx(2, l), m_ctx, alpha, bk=d, name="out_proj_ctx")
            xc, hc = _ln(yc, ln1_g[l], ln1_b[l], mod=(cx(4, l), cx(3, l)), rows_per_group=m_ctx)

        def ffn(h, xres, gate, rows_per_group, s):
            mrows = h.shape[0]
            u = _matmul(h, w_up_b, bm=_pick(rows_per_group, 1024, 8), bn=_pick(2 * ffp, 1024, LANES), bk=d,
                        out_dtype=BF16, epilogue=_identity, name="ffn_up")
            act = _conv_act(u, conv_w_p[l], conv_b_p[l], mrows // s, s)
            return _matmul_residual(act, w_down_b, xres, gate, rows_per_group, alpha, bk=bk_ff,
                                    name="ffn_down")

        yl = ffn(hl, xl, lat(5, l), seq, seq)
        if last:
            xl = _ln(yl, ln2_g[l], ln2_b[l])
        else:
            xl, hl = _ln(yl, ln2_g[l], ln2_b[l], mod=(lat(1, l + 1), lat(0, l + 1)), rows_per_group=seq)
            yc = ffn(hc, xc, cx(5, l), m_ctx, n_ctx)
            xc, hc = _ln(yc, ln2_g[l], ln2_b[l], mod=(cx(1, l + 1), cx(0, l + 1)), rows_per_group=m_ctx)

        if not last:
            (w_in_b,) = w_in_next

    return xl.reshape(batch, seq, d)
```

```python
import functools
import math

import jax
import jax.numpy as jnp
from jax import lax
from jax.experimental import pallas as pl
from jax.experimental.pallas import tpu as pltpu

F32 = jnp.float32
BF16 = jnp.bfloat16

GRID_W = 64
D_HEAD = 128
H_DIFF = 8
DIFF_W = 2 * D_HEAD
FNET_GROUPS = 8
FNET_GROUP_W = 256
H_GQA = 16
H_KV = 4
GQA_GROUP = H_GQA // H_KV
GQA_W = GQA_GROUP * D_HEAD
BRANCH_W = 2048
N_BRANCH = 3
ROPE_THETA = 10000.0
NORM_EPS = 1e-6

OFF_AQ = 0
OFF_AK = OFF_AQ + H_DIFF * DIFF_W
OFF_AV = OFF_AK + H_DIFF * DIFF_W
OFF_B = OFF_AV + H_DIFF * DIFF_W
OFF_CQ = OFF_B + FNET_GROUPS * FNET_GROUP_W
OFF_CK = OFF_CQ + H_GQA * D_HEAD
OFF_CV = OFF_CK + H_KV * D_HEAD
D_IN = OFF_CV + H_KV * D_HEAD
Q_SCALE = D_HEAD ** -0.5 * math.log2(math.e)

V7X_VMEM_LIMIT_BYTES = 56 * 1024 * 1024
LANES = 128
IN_TILE = 512
ATTN_ROWS = 256


def _params(sem):
    return pltpu.CompilerParams(dimension_semantics=sem, vmem_limit_bytes=V7X_VMEM_LIMIT_BYTES)


def _pick(dim, pref, align):
    t = (min(pref, dim) // align) * align
    while t >= align:
        if dim % t == 0:
            return t
        t -= align
    return dim


def _dot(a, b):
    return jnp.dot(a, b, preferred_element_type=F32)


def _dot_nt(a, b):
    return lax.dot_general(a, b, (((1,), (1,)), ((), ())), preferred_element_type=F32)


def _pipelined(n_chains, issue, finish, lookahead=1):
    pending = [issue(c) for c in range(min(lookahead, n_chains))]
    for c in range(n_chains):
        if c + lookahead < n_chains:
            pending.append(issue(c + lookahead))
        finish(c, pending.pop(0))


class _Cast:
    def __init__(self, w, layer, max_blocks, *, nseg=1, cp=None, rp=None, tile=None):
        _, r, c_all = w.shape
        self.w, self.layer, self.nseg, self.tile = w, layer, nseg, tile
        self.c = c_all // nseg
        self.cp = cp or self.c
        rows_out = rp or r
        div = math.gcd(r, rows_out)
        self.rb = next(rb for rb in range(16, div + 1, 16) if div % rb == 0 and rows_out // rb <= max_blocks)
        self.n_in, self.n_out = r // self.rb, rows_out // self.rb
        self.out_block = (c_all // tile, self.rb, tile) if tile else (1, self.rb, nseg * self.cp)
        self.out_shape = jax.ShapeDtypeStruct((self.out_block[0], rows_out, self.out_block[2]), BF16)

    def specs(self, step):
        out_blk = lambda *g: jnp.minimum(step(*g), self.n_out - 1)
        return (pl.BlockSpec((1, self.rb, self.nseg * self.c),
                             lambda *g: (self.layer, jnp.minimum(out_blk(*g), self.n_in - 1), 0)),
                pl.BlockSpec(self.out_block, lambda *g: (0, out_blk(*g), 0)))

    def body(self, t, x_ref, o_ref):
        x = x_ref[0]
        if self.n_out > self.n_in:
            x = jnp.where(jnp.minimum(t, self.n_out - 1) < self.n_in, x, 0.0)
        if self.tile:
            for n in range(self.out_block[0]):
                o_ref[n] = x[:, n * self.tile:(n + 1) * self.tile].astype(BF16)
            return
        for s in range(self.nseg):
            o_ref[0, :, s * self.cp:s * self.cp + self.c] = x[:, s * self.c:(s + 1) * self.c].astype(BF16)
            if self.cp > self.c:
                o_ref[0, :, s * self.cp + self.c:(s + 1) * self.cp] = jnp.zeros((self.rb, self.cp - self.c), BF16)


def _run_casts(t, casts, refs):
    for n, cast in enumerate(casts):
        cast.body(t, refs[n], refs[len(casts) + n])


def _cast_kernel(x_ref, o_ref, *, cast):
    cast.body(pl.program_id(0), x_ref, o_ref)


def _cast_now(cast):
    in_spec, out_spec = cast.specs(lambda t: t)
    return pl.pallas_call(
        functools.partial(_cast_kernel, cast=cast), grid=(cast.n_out,), in_specs=[in_spec],
        out_specs=out_spec, out_shape=cast.out_shape,
        compiler_params=_params(("arbitrary",)), name="cast_weight",
    )(cast.w)


def _bf16_pieces(x, n):
    pieces = []
    for _ in range(n):
        p = x.astype(BF16).astype(F32)
        pieces.append(p)
        x = x - p
    return pieces


def _ada_kernel(c_ref, w_ref, b_ref, o_ref):
    c = c_ref[...]
    rows = c.shape[0]
    s3 = jnp.concatenate(_bf16_pieces(c * jax.nn.sigmoid(c), 3), axis=0).astype(BF16)
    w_hi, w_lo = _bf16_pieces(w_ref[0], 2)
    hi = _dot(s3, w_hi.astype(BF16))
    lo = _dot(s3, w_lo.astype(BF16))
    acc = (hi[:rows] + hi[rows:2 * rows] + hi[2 * rows:]) + (lo[:rows] + lo[rows:2 * rows])
    o_ref[0] = acc + b_ref[0]


def _ada(cvec, w_ada, b_ada):
    depth, d, n = w_ada.shape
    bn = _pick(n, 512, LANES)
    return pl.pallas_call(
        _ada_kernel,
        grid=(depth, n // bn),
        in_specs=[pl.BlockSpec((8, d), lambda l, j: (0, 0)),
                  pl.BlockSpec((1, d, bn), lambda l, j: (l, 0, j)),
                  pl.BlockSpec((1, 1, bn), lambda l, j: (l, 0, j))],
        out_specs=pl.BlockSpec((1, 8, bn), lambda l, j: (l, 0, j)),
        out_shape=jax.ShapeDtypeStruct((depth, 8, n), F32),
        compiler_params=_params(("arbitrary", "arbitrary")),
        name="ada",
    )(cvec, w_ada, b_ada.reshape(depth, 1, n))


def _mod_kernel(x_ref, sc_ref, sh_ref, h_ref):
    h_ref[...] = (x_ref[...] * (1.0 + sc_ref[0]) + sh_ref[0]).astype(BF16)


def _modulate(x, sc, sh, rows_per_group):
    m, d = x.shape
    bs = _pick(rows_per_group, 256, 8)
    grp = lambda i: (i * bs // rows_per_group, 0, 0)
    return pl.pallas_call(
        _mod_kernel,
        grid=(m // bs,),
        in_specs=[pl.BlockSpec((bs, d), lambda i: (i, 0)),
                  pl.BlockSpec((1, 1, d), grp),
                  pl.BlockSpec((1, 1, d), grp)],
        out_specs=pl.BlockSpec((bs, d), lambda i: (i, 0)),
        out_shape=jax.ShapeDtypeStruct((m, d), BF16),
        compiler_params=_params(("arbitrary",)),
        name="modulate",
    )(x, sc, sh)


def _layer_norm(y, g, b):
    mu = jnp.mean(y, axis=-1, keepdims=True)
    dlt = y - mu
    var = jnp.mean(dlt * dlt, axis=-1, keepdims=True)
    return dlt * lax.rsqrt(var + NORM_EPS) * g + b


def _ln_mod_kernel(y_ref, g_ref, b_ref, sc_ref, sh_ref, x_ref, h_ref):
    xn = _layer_norm(y_ref[...], g_ref[...], b_ref[...])
    x_ref[...] = xn
    h_ref[...] = (xn * (1.0 + sc_ref[0]) + sh_ref[0]).astype(BF16)


def _ln_kernel(y_ref, g_ref, b_ref, x_ref):
    x_ref[...] = _layer_norm(y_ref[...], g_ref[...], b_ref[...])


def _ln(y, g, b, mod=None, rows_per_group=None):
    m, d = y.shape
    bs = _pick(rows_per_group or m, 256, 8)
    row = pl.BlockSpec((bs, d), lambda i: (i, 0))
    vec = pl.BlockSpec((1, d), lambda i: (0, 0))
    g2, b2 = g.reshape(1, d), b.reshape(1, d)
    if mod is None:
        return pl.pallas_call(
            _ln_kernel, grid=(m // bs,), in_specs=[row, vec, vec], out_specs=row,
            out_shape=jax.ShapeDtypeStruct((m, d), F32),
            compiler_params=_params(("arbitrary",)), name="layernorm",
        )(y, g2, b2)
    grp = pl.BlockSpec((1, 1, d), lambda i: (i * bs // rows_per_group, 0, 0))
    return pl.pallas_call(
        _ln_mod_kernel, grid=(m // bs,), in_specs=[row, vec, vec, grp, grp],
        out_specs=[row, row],
        out_shape=[jax.ShapeDtypeStruct((m, d), F32), jax.ShapeDtypeStruct((m, d), BF16)],
        compiler_params=_params(("arbitrary",)), name="layernorm_mod",
    )(y, g2, b2, mod[0], mod[1])


def _mm_kernel(*refs, nk, n_extra, epilogue):
    x_ref, w_ref = refs[0], refs[1]
    extra = refs[2:2 + n_extra]
    o_ref = refs[2 + n_extra]
    part = _dot(x_ref[...], w_ref[0])
    if nk == 1:
        o_ref[...] = epilogue(part, *extra).astype(o_ref.dtype)
        return
    acc_ref = refs[3 + n_extra]
    k = pl.program_id(2)

    @pl.when(k == 0)
    def _():
        acc_ref[...] = part

    @pl.when((k > 0) & (k < nk - 1))
    def _():
        acc_ref[...] += part

    @pl.when(k == nk - 1)
    def _():
        o_ref[...] = epilogue(acc_ref[...] + part, *extra).astype(o_ref.dtype)


def _matmul(x, w, *, bm, bn, bk, out_dtype, epilogue, extra=(), extra_specs=(), name):
    m, kdim = x.shape
    n = w.shape[2]
    nk = kdim // bk
    kernel = functools.partial(_mm_kernel, nk=nk, n_extra=len(extra), epilogue=epilogue)
    return pl.pallas_call(
        kernel,
        grid=(m // bm, n // bn, nk),
        in_specs=[pl.BlockSpec((bm, bk), lambda i, j, k: (i, k)),
                  pl.BlockSpec((1, bk, bn), lambda i, j, k: (0, k, j)),
                  *extra_specs],
        out_specs=pl.BlockSpec((bm, bn), lambda i, j, k: (i, j)),
        out_shape=jax.ShapeDtypeStruct((m, n), out_dtype),
        scratch_shapes=[pltpu.VMEM((bm, bn), F32)] if nk > 1 else [],
        compiler_params=_params(("arbitrary", "arbitrary", "arbitrary")),
        name=name,
    )(x, w, *extra)


def _identity(acc):
    return acc


def _residual_epilogue(acc, x_ref, gate_ref, *, alpha):
    return alpha * x_ref[...] + gate_ref[0] * acc


def _matmul_residual(a, w, xres, gate, rows_per_group, alpha, *, bk, name):
    n = w.shape[2]
    bm = _pick(rows_per_group, 1024, 8)
    bn = _pick(n, 1024, LANES)
    return _matmul(
        a, w, bm=bm, bn=bn, bk=bk, out_dtype=F32,
        epilogue=functools.partial(_residual_epilogue, alpha=alpha),
        extra=(xres, gate),
        extra_specs=(pl.BlockSpec((bm, bn), lambda i, j, k: (i, j)),
                     pl.BlockSpec((1, 1, bn), lambda i, j, k: (i * bm // rows_per_group, 0, j))),
        name=name)


def _rope(x, cos, sin_signed):
    return x * cos + pltpu.roll(x, D_HEAD // 2, 1) * sin_signed


def _inproj_kernel(h_ref, w_ref, cos_ref, sin_ref, g_ref, *refs, casts, n_col_tiles):
    o_ref = refs[len(casts)]
    i, j = pl.program_id(0), pl.program_id(1)
    _run_casts(i * n_col_tiles + j, casts, refs[:len(casts)] + refs[len(casts) + 1:])
    acc = _dot(h_ref[...], w_ref[0])
    n_chunks = IN_TILE // D_HEAD
    t_ak, t_av, t_cq, t_ck, t_cv = (OFF_AK // IN_TILE, OFF_AV // IN_TILE, OFF_CQ // IN_TILE,
                                    OFF_CK // IN_TILE, OFF_CV // IN_TILE)

    @pl.when(j < t_av)
    def _():
        s = jnp.where(j < t_ak, Q_SCALE, 1.0).astype(F32)
        cos = cos_ref[...] * s
        sin = sin_ref[...] * s
        for c in range(n_chunks):
            sl = slice(c * D_HEAD, (c + 1) * D_HEAD)
            o_ref[:, sl] = _rope(acc[:, sl], cos, sin).astype(BF16)

    @pl.when(((j >= t_av) & (j < t_cq)) | (j >= t_cv))
    def _():
        o_ref[...] = acc.astype(BF16)

    @pl.when((j >= t_cq) & (j < t_cv))
    def _():
        is_q = j < t_ck
        g = jnp.where(is_q, g_ref[0:1, :], g_ref[1:2, :])
        s = jnp.where(is_q, Q_SCALE, 1.0).astype(F32)
        cos = cos_ref[...] * s
        sin = sin_ref[...] * s
        for c in range(n_chunks):
            sl = slice(c * D_HEAD, (c + 1) * D_HEAD)
            x = acc[:, sl]
            ms = jnp.mean(x * x, axis=-1, keepdims=True)
            xn = x * lax.rsqrt(ms + NORM_EPS) * g
            o_ref[:, sl] = _rope(xn, cos, sin).astype(BF16)


def _inproj_grid(m, rows_per_seq):
    bm = _pick(rows_per_seq, 1024, 8)
    return bm, (m // bm, D_IN // IN_TILE)


def _inproj(h, w, cos, sin_signed, qk_g, rows_per_seq, casts=()):
    m, d = h.shape
    bm, grid = _inproj_grid(m, rows_per_seq)
    nseq_blk = rows_per_seq // bm
    tab = pl.BlockSpec((bm, D_HEAD), lambda i, j: (i % nseq_blk, 0))
    cast_specs = [cast.specs(lambda i, j: i * grid[1] + j) for cast in casts]
    return pl.pallas_call(
        functools.partial(_inproj_kernel, casts=tuple(casts), n_col_tiles=grid[1]),
        grid=grid,
        in_specs=[pl.BlockSpec((bm, d), lambda i, j: (i, 0)),
                  pl.BlockSpec((1, d, IN_TILE), lambda i, j: (0, 0, j)),
                  tab, tab,
                  pl.BlockSpec((2, D_HEAD), lambda i, j: (0, 0)),
                  *[sp[0] for sp in cast_specs]],
        out_specs=[pl.BlockSpec((bm, IN_TILE), lambda i, j: (i, j)), *[sp[1] for sp in cast_specs]],
        out_shape=[jax.ShapeDtypeStruct((m, D_IN), BF16), *[cast.out_shape for cast in casts]],
        compiler_params=_params(("arbitrary", "arbitrary")),
        name="inproj",
    )(h, w, cos, sin_signed, qk_g, *[cast.w for cast in casts])


KEY_CHUNK = 1024


class _SoftmaxChain:
    def __init__(self, q, k_refs, k_lo, values, ones_at=None):
        self.q, self.k_refs, self.k_lo, self.values, self.ones_at = q, k_refs, k_lo, values, ones_at
        self.chunks = [(n, c0, min(c0 + KEY_CHUNK, k_ref.shape[0]))
                       for n, k_ref in enumerate(k_refs) for c0 in range(0, k_ref.shape[0], KEY_CHUNK)]
        self.scores = []
        self.lane_max = self.lane_sum = self.acc = self.row_max = None

    def score(self, i):
        n, c0, c1 = self.chunks[i]
        s = _dot_nt(self.q, self.k_refs[n][c0:c1, self.k_lo:self.k_lo + D_HEAD])
        self.scores.append(s)
        for t in range(0, c1 - c0, LANES):
            tile = s[:, t:t + LANES]
            self.lane_max = tile if self.lane_max is None else jnp.maximum(self.lane_max, tile)

    def accumulate(self, i):
        if self.row_max is None:
            self.row_max = jnp.max(self.lane_max, axis=-1, keepdims=True)
        n, c0, c1 = self.chunks[i]
        x = self.scores[i] - self.row_max
        if self.ones_at is None:
            e = jnp.exp2(x)
            for t in range(0, c1 - c0, LANES):
                tile = e[:, t:t + LANES]
                self.lane_sum = tile if self.lane_sum is None else self.lane_sum + tile
            e = e.astype(BF16)
        else:
            e = jnp.exp2(x.astype(BF16))
        pv = _dot(e, self.values[n][c0:c1])
        self.acc = pv if self.acc is None else self.acc + pv

    def result(self):
        if self.ones_at is None:
            return self.acc, jnp.sum(self.lane_sum, axis=-1, keepdims=True)
        return self.acc[:, :self.ones_at], self.acc[:, self.ones_at:self.ones_at + 1]


def _run_chains(n_blocks, make_chains, write):
    cur = make_chains(0)
    for i in range(len(cur[0].chunks)):
        for ch in cur:
            ch.score(i)
    for b in range(n_blocks):
        nxt = make_chains(b + 1) if b + 1 < n_blocks else []
        for i in range(len(cur[0].chunks)):
            for ch in nxt:
                ch.score(i)
            for ch in cur:
                ch.accumulate(i)
        write(b, cur)
        cur = nxt


def _diff_attn_kernel(*refs, n_src, lam_init):
    q_ref = refs[0]
    k_refs = refs[1:1 + n_src]
    v_refs = refs[1 + n_src:1 + 2 * n_src]
    lam_ref, g_ref, o_ref = refs[1 + 2 * n_src:]
    lp = lam_ref[...]
    lam = (jnp.exp(jnp.sum(lp[0:1] * lp[1:2], axis=-1, keepdims=True))
           - jnp.exp(jnp.sum(lp[2:3] * lp[3:4], axis=-1, keepdims=True)) + lam_init)
    rows = min(ATTN_ROWS, q_ref.shape[0])

    def make_chains(b):
        q = q_ref[b * rows:(b + 1) * rows, :]
        return [_SoftmaxChain(q[:, lo:lo + D_HEAD], k_refs, lo, v_refs) for lo in (0, D_HEAD)]

    def write(b, chains):
        (o1, den1), (o2, den2) = chains[0].result(), chains[1].result()
        o = o1 * (1.0 / den1) - o2 * (lam / den2)
        ms = jnp.mean(o * o, axis=-1, keepdims=True)
        y = (o * lax.rsqrt(ms + NORM_EPS)) * g_ref[...] * (1.0 - lam_init)
        o_ref[b * rows:(b + 1) * rows, :] = y.astype(BF16)

    _run_chains(q_ref.shape[0] // rows, make_chains, write)


def _diff_attention(pq, kv_sources, lam_p, subln_g, lam_init, batch, q_len, bq):
    nq = q_len // bq
    kblk, vblk = OFF_AK // DIFF_W, OFF_AV // DIFF_W
    k_specs = [pl.BlockSpec((n, DIFF_W), lambda b, h, i: (b, kblk + h)) for _, n in kv_sources]
    v_specs = [pl.BlockSpec((n, DIFF_W), lambda b, h, i: (b, vblk + h)) for _, n in kv_sources]
    srcs = [p for p, _ in kv_sources]
    kernel = functools.partial(_diff_attn_kernel, n_src=len(srcs), lam_init=lam_init)
    return pl.pallas_call(
        kernel,
        grid=(batch, H_DIFF, nq),
        in_specs=[pl.BlockSpec((bq, DIFF_W), lambda b, h, i: (b * nq + i, h)),
                  *k_specs, *v_specs,
                  pl.BlockSpec((4, D_HEAD), lambda b, h, i: (0, 0)),
                  pl.BlockSpec((1, DIFF_W), lambda b, h, i: (0, 0))],
        out_specs=pl.BlockSpec((bq, DIFF_W), lambda b, h, i: (b * nq + i, h)),
        out_shape=jax.ShapeDtypeStruct((batch * q_len, H_DIFF * DIFF_W), BF16),
        compiler_params=_params(("arbitrary", "arbitrary", "arbitrary")),
        name="diff_attention",
    )(pq, *srcs, *srcs, lam_p, subln_g.reshape(1, DIFF_W))


def _gqa_kernel(*refs, n_src):
    q_ref = refs[0]
    k_refs = refs[1:1 + n_src]
    v_refs = refs[1 + n_src:1 + 2 * n_src]
    o_ref = refs[1 + 2 * n_src]
    rows = min(ATTN_ROWS // 2, q_ref.shape[0])
    heads = [slice(g * D_HEAD, (g + 1) * D_HEAD) for g in range(GQA_GROUP)]
    values = [jnp.concatenate([v_ref[...], jnp.ones(v_ref.shape, BF16)], axis=1) for v_ref in v_refs]

    def make_chains(b):
        rs = slice(b * rows, (b + 1) * rows)
        q = jnp.concatenate([q_ref[rs, cs] for cs in heads], axis=0)
        return [_SoftmaxChain(q, k_refs, 0, values, ones_at=D_HEAD)]

    def write(b, chains):
        o, den = chains[0].result()
        o = (o / den).astype(BF16)
        for g, cs in enumerate(heads):
            o_ref[b * rows:(b + 1) * rows, cs] = o[g * rows:(g + 1) * rows]

    _run_chains(q_ref.shape[0] // rows, make_chains, write)


def _gqa_attention(pq, kv_sources, batch, q_len, bq):
    nq = q_len // bq
    qblk, kblk, vblk = OFF_CQ // GQA_W, OFF_CK // D_HEAD, OFF_CV // D_HEAD
    k_specs = [pl.BlockSpec((n, D_HEAD), lambda b, kh, i: (b, kblk + kh)) for _, n in kv_sources]
    v_specs = [pl.BlockSpec((n, D_HEAD), lambda b, kh, i: (b, vblk + kh)) for _, n in kv_sources]
    srcs = [p for p, _ in kv_sources]
    return pl.pallas_call(
        functools.partial(_gqa_kernel, n_src=len(srcs)),
        grid=(batch, H_KV, nq),
        in_specs=[pl.BlockSpec((bq, GQA_W), lambda b, kh, i: (b * nq + i, qblk + kh)),
                  *k_specs, *v_specs],
        out_specs=pl.BlockSpec((bq, GQA_W), lambda b, kh, i: (b * nq + i, kh)),
        out_shape=jax.ShapeDtypeStruct((batch * q_len, H_GQA * D_HEAD), BF16),
        compiler_params=_params(("arbitrary",) * 3),
        name="gqa_attention",
    )(pq, *srcs, *srcs)


def _dft_tables(n, split=64):
    k = jnp.arange(n, dtype=jnp.int32)

    def cos_sin(t):
        ang = ((k[:, None] * t[None, :]) % n).astype(F32) * (2.0 * math.pi / n)
        return jnp.cos(ang), jnp.sin(ang)

    if n <= split or n % split:
        c, s = cos_sin(k)
    else:
        ca, sa = (v[:, :, None] for v in cos_sin(jnp.arange(0, n, split, dtype=jnp.int32)))
        cb, sb = (v[:, None, :] for v in cos_sin(jnp.arange(split, dtype=jnp.int32)))
        c = (ca * cb - sa * sb).reshape(n, n)
        s = (sa * cb + ca * sb).reshape(n, n)
    return c.astype(BF16), s.astype(BF16)


FNET_RADIX = 4


def _fnet_twiddles(n):
    p = jnp.arange(1, FNET_RADIX, dtype=jnp.int32)[:, None]
    k = jnp.arange(n // FNET_RADIX, dtype=jnp.int32)[None, :]
    ang = ((p * k) % n).astype(F32) * (2.0 * math.pi / n)
    return jnp.cos(ang)[:, :, None], jnp.sin(ang)[:, :, None]


def _fnet_kernel(c_ref, s_ref, x0_ref, x1_ref, x2_ref, x3_ref, twc_ref, tws_ref, cc_ref, sc_ref, o_ref, *, scale):
    c, s = c_ref[...], s_ref[...]
    ys = []
    for p, x_ref in enumerate((x0_ref, x1_ref, x2_ref, x3_ref)):
        x = x_ref[...]
        zr, zi = _dot(c, x), -_dot(s, x)
        if p:
            tc, ts = twc_ref[p - 1], tws_ref[p - 1]
            zr, zi = zr * tc + zi * ts, zi * tc - zr * ts
        ys.append((zr, zi))
    (ar, ai), (br, bi), (cr, ci), (dr, di) = ys
    zs = ((ar + br + cr + dr, ai + bi + ci + di),
          (ar + bi - cr - di, ai - br - ci + dr),
          (ar - br + cr - dr, ai - bi + ci - di),
          (ar - bi - cr + di, ai + br - ci - dr))
    for q, (zr, zi) in enumerate(zs):
        zr, zi = zr.astype(BF16), zi.astype(BF16)
        for g in range(zr.shape[1] // FNET_GROUP_W):
            sl = slice(g * FNET_GROUP_W, (g + 1) * FNET_GROUP_W)
            y = _dot(zr[:, sl], cc_ref[...]) + _dot(zi[:, sl], sc_ref[...])
            o_ref[0, q, :, sl] = (y * scale).astype(BF16)


def _fourier_mix(p, batch, n, pos_tabs, twiddles, chan_tabs):
    nr = n // FNET_RADIX
    bk = _pick(nr, 512, 16)
    bc = 2 * FNET_GROUP_W
    width = FNET_GROUPS * FNET_GROUP_W
    pos = pl.BlockSpec((bk, nr), lambda b, i, j: (i, 0))
    tw = pl.BlockSpec((FNET_RADIX - 1, bk, 1), lambda b, i, j: (0, i, 0))
    chan = pl.BlockSpec((FNET_GROUP_W, FNET_GROUP_W), lambda b, i, j: (0, 0))
    p4 = p[:, OFF_B:OFF_B + width].reshape(batch * nr, FNET_RADIX * width)
    xs = [pl.BlockSpec((nr, bc), lambda b, i, j, r=r: (b, r * width // bc + j)) for r in range(FNET_RADIX)]
    scale = 1.0 / math.sqrt(n * FNET_GROUP_W)
    out = pl.pallas_call(
        functools.partial(_fnet_kernel, scale=scale),
        grid=(batch, nr // bk, width // bc),
        in_specs=[pos, pos, *xs, tw, tw, chan, chan],
        out_specs=pl.BlockSpec((1, FNET_RADIX, bk, bc), lambda b, i, j: (b, 0, i, j)),
        out_shape=jax.ShapeDtypeStruct((batch, FNET_RADIX, nr, width), BF16),
        compiler_params=_params(("arbitrary",) * 3),
        name="fourier_mix",
    )(*pos_tabs, p4, p4, p4, p4, *twiddles, *chan_tabs)
    return out.reshape(batch * n, width)


MERGE_TILE = 256


def _merge_kernel(h_ref, oa_ref, ob_ref, oc_ref, wg0_ref, wg1_ref, wg2_ref, bg0_ref, bg1_ref, bg2_ref,
                  wb0_ref, wb1_ref, wb2_ref, o_ref):
    h = h_ref[...]
    acc = None
    for br_ref, wg_ref, bg_ref, wb_ref in ((oa_ref, wg0_ref, bg0_ref, wb0_ref), (ob_ref, wg1_ref, bg1_ref, wb1_ref),
                                           (oc_ref, wg2_ref, bg2_ref, wb2_ref)):
        gate = jax.nn.sigmoid(_dot(h, wg_ref[0]) + bg_ref[0])
        term = gate * _dot(br_ref[...], wb_ref[0])
        acc = term if acc is None else acc + term
    o_ref[...] = acc.astype(BF16)


def _gated_merge(h, oa, ob, oc, w_gate, b_gate, w_branch, layer):
    m, d = h.shape
    bm = _pick(m, 1024, 8)
    bn = MERGE_TILE
    nj = d // bn
    once = pl.Buffered(1)
    br = pl.BlockSpec((bm, BRANCH_W), lambda i, j: (i, 0), pipeline_mode=once)
    wg = [pl.BlockSpec((1, d, bn), lambda i, j, n=n: (n * nj + j, 0, 0)) for n in range(N_BRANCH)]
    wb = [pl.BlockSpec((1, BRANCH_W, bn), lambda i, j, n=n: (j, n, 0)) for n in range(N_BRANCH)]
    bg = [pl.BlockSpec((1, 1, bn), lambda i, j, n=n: (layer, 0, n * nj + j)) for n in range(N_BRANCH)]
    return pl.pallas_call(
        _merge_kernel,
        grid=(m // bm, nj),
        in_specs=[pl.BlockSpec((bm, d), lambda i, j: (i, 0), pipeline_mode=once), br, br, br, *wg, *bg, *wb],
        out_specs=pl.BlockSpec((bm, bn), lambda i, j: (i, j)),
        out_shape=jax.ShapeDtypeStruct((m, d), BF16),
        compiler_params=_params(("arbitrary",) * 2),
        name="gated_merge",
    )(h, oa, ob, oc, w_gate, w_gate, w_gate, b_gate, b_gate, b_gate, w_branch, w_branch, w_branch)


CONV_EDGE_ROWS = 16


def _conv_act_kernel(a_ref, g_ref, w_ref, b_ref, o_ref):
    w, bias = w_ref[...], b_ref[...]
    s = a_ref.shape[1]
    edge = 2 * CONV_EDGE_ROWS

    def act(rows, first, last):
        a = a_ref[0, rows, :].astype(F32)
        n = a.shape[0]
        prev, nxt = pltpu.roll(a, 1, 0), pltpu.roll(a, n - 1, 0)
        row = lax.broadcasted_iota(jnp.int32, a.shape, 0)
        if first:
            prev = jnp.where(row == 0, 0.0, prev)
        if last:
            nxt = jnp.where(row == n - 1, 0.0, nxt)
        z = prev * w[0:1] + a * w[1:2] + nxt * w[2:3] + bias
        gelu = 0.5 * z * (1.0 + lax.erf(z * math.sqrt(0.5)))
        return (gelu * g_ref[0, rows, :].astype(F32)).astype(BF16)

    o_ref[0] = act(slice(0, s), False, False)
    o_ref[0, :CONV_EDGE_ROWS, :] = act(slice(0, edge), True, False)[:CONV_EDGE_ROWS]
    o_ref[0, s - CONV_EDGE_ROWS:, :] = act(slice(s - edge, s), False, True)[CONV_EDGE_ROWS:]


def _conv_act(u, conv_w, conv_b, batch, s):
    ffp = conv_w.shape[1]
    bc = _pick(ffp, 256, LANES)
    nj = ffp // bc
    u3 = u.reshape(batch, s, 2 * ffp)
    out = pl.pallas_call(
        _conv_act_kernel,
        grid=(batch, nj),
        in_specs=[pl.BlockSpec((1, s, bc), lambda b, j: (b, 0, j)),
                  pl.BlockSpec((1, s, bc), lambda b, j: (b, 0, nj + j)),
                  pl.BlockSpec((3, bc), lambda b, j: (0, j)),
                  pl.BlockSpec((1, bc), lambda b, j: (0, j))],
        out_specs=pl.BlockSpec((1, s, bc), lambda b, j: (b, 0, j)),
        out_shape=jax.ShapeDtypeStruct((batch, s, ffp), BF16),
        compiler_params=_params(("arbitrary", "arbitrary")),
        name="conv_act",
    )(u3, u3, conv_w, conv_b)
    return out.reshape(batch * s, ffp)


def _rope_tables(n):
    n_rows = n // GRID_W
    rows = jnp.broadcast_to(jnp.arange(n_rows)[:, None], (n_rows, GRID_W)).reshape(-1).astype(F32)
    cols = jnp.broadcast_to(jnp.arange(GRID_W)[None, :], (n_rows, GRID_W)).reshape(-1).astype(F32)
    quarter = D_HEAD // 4
    inv_freq = ROPE_THETA ** (-jnp.arange(quarter, dtype=F32) / quarter)
    ang = jnp.concatenate([rows[:, None] * inv_freq, cols[:, None] * inv_freq], axis=-1)
    ang = jnp.concatenate([ang, ang], axis=-1)
    sign = jnp.where(jnp.arange(D_HEAD) < D_HEAD // 2, -1.0, 1.0).astype(F32)
    return jnp.cos(ang), jnp.sin(ang) * sign


def kernel(x, c, ctx, c_ctx, w_ada, b_ada, w_in, diff_lambda, diff_subln_g, qk_norm_g, w_branch, w_gate,
           b_gate, w_o, ln1_g, ln1_b, w_up, conv_w, conv_b, w_down, ln2_g, ln2_b):
    batch, seq, d = x.shape
    n_ctx = ctx.shape[1]
    depth = w_ada.shape[0]
    ff = conv_w.shape[2]
    assert ff % LANES == 0 and w_in.shape[2] == D_IN
    ffp = -(-ff // 1024) * 1024 if ff > 1024 else ff
    alpha = (2 * depth) ** 0.25

    cvec = jnp.zeros((8, d), F32).at[:batch].set(c).at[batch].set(c_ctx)
    mod = _ada(cvec, w_ada, b_ada).reshape(depth, 8, 6, d)

    pad = ffp - ff
    conv_w_p = jnp.pad(conv_w, ((0, 0), (0, 0), (0, pad)))
    conv_b_p = jnp.pad(conv_b, ((0, 0), (0, pad))).reshape(depth, 1, ffp)
    b_gate3 = b_gate.reshape(depth, 1, N_BRANCH * d)
    w_branch2 = w_branch.reshape(depth, N_BRANCH * BRANCH_W, d)

    _, inproj_grid = _inproj_grid(batch * seq, seq)
    n_steps = inproj_grid[0] * inproj_grid[1]

    def layer_casts(l):
        casts = [_Cast(w_gate, l, n_steps, tile=MERGE_TILE), _Cast(w_branch2, l, n_steps, tile=MERGE_TILE),
                 _Cast(w_o, l, n_steps),
                 _Cast(w_up, l, n_steps, nseg=2, cp=ffp), _Cast(w_down, l, n_steps, rp=ffp)]
        if l + 1 < depth:
            casts.append(_Cast(w_in, l + 1, n_steps))
        return casts

    w_in_b = _cast_now(_Cast(w_in, 0, 32))

    cos_l, sin_l = _rope_tables(seq)
    cos_c, sin_c = jnp.ones((batch * n_ctx, D_HEAD), F32), jnp.zeros((batch * n_ctx, D_HEAD), F32)
    pos_l, pos_c = _dft_tables(seq // FNET_RADIX), _dft_tables(n_ctx // FNET_RADIX)
    tw_l, tw_c = _fnet_twiddles(seq), _fnet_twiddles(n_ctx)
    chan = _dft_tables(FNET_GROUP_W)

    xl = x.reshape(batch * seq, d)
    xc = ctx.reshape(batch * n_ctx, d)
    m_ctx = batch * n_ctx
    bq_diff_l = _pick(seq, 8 * ATTN_ROWS, 8)
    bq_gqa_l = _pick(seq, 4 * ATTN_ROWS, 8)
    bq_c = _pick(n_ctx, ATTN_ROWS, 8)
    bk_ff = _pick(ffp, 2816, LANES)

    def lat(i, l):
        return mod[l, :batch, i][:, None, :]

    def cx(i, l):
        return mod[l, batch, i][None, None, :]

    hl = _modulate(xl, lat(1, 0), lat(0, 0), seq)
    hc = _modulate(xc, cx(1, 0), cx(0, 0), m_ctx)

    for l in range(depth):
        last = l == depth - 1
        lam_init = 0.8 - 0.6 * math.exp(-0.3 * l)

        pl_, w_gate_b, w_branch_b, w_o_b, w_up_b, w_down_b, *w_in_next = _inproj(
            hl, w_in_b, cos_l, sin_l, qk_norm_g[l], seq, casts=layer_casts(l))
        (pc_,) = _inproj(hc, w_in_b, cos_c, sin_c, qk_norm_g[l], m_ctx)
        kv = [(pl_, seq), (pc_, n_ctx)]
        oa = _diff_attention(pl_, kv, diff_lambda[l], diff_subln_g[l], lam_init, batch, seq, bq_diff_l)
        ob = _fourier_mix(pl_, batch, seq, pos_l, tw_l, chan)
        oc = _gqa_attention(pl_, kv, batch, seq, bq_gqa_l)
        merged = _gated_merge(hl, oa, ob, oc, w_gate_b, b_gate3, w_branch_b, l)
        yl = _matmul_residual(merged, w_o_b, xl, lat(2, l), seq, alpha, bk=d, name="out_proj")
        xl, hl = _ln(yl, ln1_g[l], ln1_b[l], mod=(lat(4, l), lat(3, l)), rows_per_group=seq)

        if not last:
            kvc = [(pc_, n_ctx)]
            oa_c = _diff_attention(pc_, kvc, diff_lambda[l], diff_subln_g[l], lam_init, batch, n_ctx, bq_c)
            ob_c = _fourier_mix(pc_, batch, n_ctx, pos_c, tw_c, chan)
            oc_c = _gqa_attention(pc_, kvc, batch, n_ctx, bq_c)
            merged_c = _gated_merge(hc, oa_c, ob_c, oc_c, w_gate_b, b_gate3, w_branch_b, l)
            yc = _matmul_residual(merged_c, w_o_b, xc, cx(2, l), m_ctx, alpha, bk=d, name="out_proj_ctx")
            xc, hc = _ln(yc, ln1_g[l], ln1_b[l], mod=(cx(4, l), cx(3, l)), rows_per_group=m_ctx)

        def ffn(h, xres, gate, rows_per_group, s):
            mrows = h.shape[0]
            u = _matmul(h, w_up_b, bm=_pick(rows_per_group, 1024, 8), bn=_pick(2 * ffp, 1024, LANES), bk=d,
                        out_dtype=BF16, epilogue=_identity, name="ffn_up")
            act = _conv_act(u, conv_w_p[l], conv_b_p[l], mrows // s, s)
            return _matmul_residual(act, w_down_b, xres, gate, rows_per_group, alpha, bk=bk_ff,
                                    name="ffn_down")

        yl = ffn(hl, xl, lat(5, l), seq, seq)
        if last:
            xl = _ln(yl, ln2_g[l], ln2_b[l])
        else:
            xl, hl = _ln(yl, ln2_g[l], ln2_b[l], mod=(lat(1, l + 1), lat(0, l + 1)), rows_per_group=seq)
            yc = ffn(hc, xc, cx(5, l), m_ctx, n_ctx)
            xc, hc = _ln(yc, ln2_g[l], ln2_b[l], mod=(cx(1, l + 1), cx(0, l + 1)), rows_per_group=m_ctx)

        if not last:
            (w_in_b,) = w_in_next

    return xl.reshape(batch, seq, d)
```

```python
import functools
import math

import jax
import jax.numpy as jnp
from jax import lax
from jax.experimental import pallas as pl
from jax.experimental.pallas import tpu as pltpu

F32 = jnp.float32
BF16 = jnp.bfloat16

GRID_W = 64
D_HEAD = 128
H_DIFF = 8
DIFF_W = 2 * D_HEAD
FNET_GROUPS = 8
FNET_GROUP_W = 256
H_GQA = 16
H_KV = 4
GQA_GROUP = H_GQA // H_KV
GQA_W = GQA_GROUP * D_HEAD
BRANCH_W = 2048
N_BRANCH = 3
ROPE_THETA = 10000.0
NORM_EPS = 1e-6

OFF_AQ = 0
OFF_AK = OFF_AQ + H_DIFF * DIFF_W
OFF_AV = OFF_AK + H_DIFF * DIFF_W
OFF_B = OFF_AV + H_DIFF * DIFF_W
OFF_CQ = OFF_B + FNET_GROUPS * FNET_GROUP_W
OFF_CK = OFF_CQ + H_GQA * D_HEAD
OFF_CV = OFF_CK + H_KV * D_HEAD
D_IN = OFF_CV + H_KV * D_HEAD
Q_SCALE = D_HEAD ** -0.5 * math.log2(math.e)

V7X_VMEM_LIMIT_BYTES = 56 * 1024 * 1024
LANES = 128
IN_TILE = 512
ATTN_ROWS = 256


def _params(sem):
    return pltpu.CompilerParams(dimension_semantics=sem, vmem_limit_bytes=V7X_VMEM_LIMIT_BYTES)


def _pick(dim, pref, align):
    t = (min(pref, dim) // align) * align
    while t >= align:
        if dim % t == 0:
            return t
        t -= align
    return dim


def _dot(a, b):
    return jnp.dot(a, b, preferred_element_type=F32)


def _dot_nt(a, b):
    return lax.dot_general(a, b, (((1,), (1,)), ((), ())), preferred_element_type=F32)


def _pipelined(n_chains, issue, finish, lookahead=1):
    pending = [issue(c) for c in range(min(lookahead, n_chains))]
    for c in range(n_chains):
        if c + lookahead < n_chains:
            pending.append(issue(c + lookahead))
        finish(c, pending.pop(0))


class _Cast:
    def __init__(self, w, layer, max_blocks, *, nseg=1, cp=None, rp=None, tile=None):
        _, r, c_all = w.shape
        self.w, self.layer, self.nseg, self.tile = w, layer, nseg, tile
        self.c = c_all // nseg
        self.cp = cp or self.c
        rows_out = rp or r
        div = math.gcd(r, rows_out)
        self.rb = next(rb for rb in range(16, div + 1, 16) if div % rb == 0 and rows_out // rb <= max_blocks)
        self.n_in, self.n_out = r // self.rb, rows_out // self.rb
        self.out_block = (c_all // tile, self.rb, tile) if tile else (1, self.rb, nseg * self.cp)
        self.out_shape = jax.ShapeDtypeStruct((self.out_block[0], rows_out, self.out_block[2]), BF16)

    def specs(self, step):
        out_blk = lambda *g: jnp.minimum(step(*g), self.n_out - 1)
        return (pl.BlockSpec((1, self.rb, self.nseg * self.c),
                             lambda *g: (self.layer, jnp.minimum(out_blk(*g), self.n_in - 1), 0)),
                pl.BlockSpec(self.out_block, lambda *g: (0, out_blk(*g), 0)))

    def body(self, t, x_ref, o_ref):
        x = x_ref[0]
        if self.n_out > self.n_in:
            x = jnp.where(jnp.minimum(t, self.n_out - 1) < self.n_in, x, 0.0)
        if self.tile:
            for n in range(self.out_block[0]):
                o_ref[n] = x[:, n * self.tile:(n + 1) * self.tile].astype(BF16)
            return
        for s in range(self.nseg):
            o_ref[0, :, s * self.cp:s * self.cp + self.c] = x[:, s * self.c:(s + 1) * self.c].astype(BF16)
            if self.cp > self.c:
                o_ref[0, :, s * self.cp + self.c:(s + 1) * self.cp] = jnp.zeros((self.rb, self.cp - self.c), BF16)


def _run_casts(t, casts, refs):
    for n, cast in enumerate(casts):
        cast.body(t, refs[n], refs[len(casts) + n])


def _cast_kernel(x_ref, o_ref, *, cast):
    cast.body(pl.program_id(0), x_ref, o_ref)


def _cast_now(cast):
    in_spec, out_spec = cast.specs(lambda t: t)
    return pl.pallas_call(
        functools.partial(_cast_kernel, cast=cast), grid=(cast.n_out,), in_specs=[in_spec],
        out_specs=out_spec, out_shape=cast.out_shape,
        compiler_params=_params(("arbitrary",)), name="cast_weight",
    )(cast.w)


def _bf16_pieces(x, n):
    pieces = []
    for _ in range(n):
        p = x.astype(BF16).astype(F32)
        pieces.append(p)
        x = x - p
    return pieces


def _ada_kernel(c_ref, w_ref, b_ref, o_ref):
    c = c_ref[...]
    rows = c.shape[0]
    s3 = jnp.concatenate(_bf16_pieces(c * jax.nn.sigmoid(c), 3), axis=0).astype(BF16)
    w_hi, w_lo = _bf16_pieces(w_ref[0], 2)
    hi = _dot(s3, w_hi.astype(BF16))
    lo = _dot(s3, w_lo.astype(BF16))
    acc = (hi[:rows] + hi[rows:2 * rows] + hi[2 * rows:]) + (lo[:rows] + lo[rows:2 * rows])
    o_ref[0] = acc + b_ref[0]


def _ada(cvec, w_ada, b_ada):
    depth, d, n = w_ada.shape
    bn = _pick(n, 512, LANES)
    return pl.pallas_call(
        _ada_kernel,
        grid=(depth, n // bn),
        in_specs=[pl.BlockSpec((8, d), lambda l, j: (0, 0)),
                  pl.BlockSpec((1, d, bn), lambda l, j: (l, 0, j)),
                  pl.BlockSpec((1, 1, bn), lambda l, j: (l, 0, j))],
        out_specs=pl.BlockSpec((1, 8, bn), lambda l, j: (l, 0, j)),
        out_shape=jax.ShapeDtypeStruct((depth, 8, n), F32),
        compiler_params=_params(("arbitrary", "arbitrary")),
        name="ada",
    )(cvec, w_ada, b_ada.reshape(depth, 1, n))


def _mod_kernel(x_ref, sc_ref, sh_ref, h_ref):
    h_ref[...] = (x_ref[...] * (1.0 + sc_ref[0]) + sh_ref[0]).astype(BF16)


def _modulate(x, sc, sh, rows_per_group):
    m, d = x.shape
    bs = _pick(rows_per_group, 256, 8)
    grp = lambda i: (i * bs // rows_per_group, 0, 0)
    return pl.pallas_call(
        _mod_kernel,
        grid=(m // bs,),
        in_specs=[pl.BlockSpec((bs, d), lambda i: (i, 0)),
                  pl.BlockSpec((1, 1, d), grp),
                  pl.BlockSpec((1, 1, d), grp)],
        out_specs=pl.BlockSpec((bs, d), lambda i: (i, 0)),
        out_shape=jax.ShapeDtypeStruct((m, d), BF16),
        compiler_params=_params(("arbitrary",)),
        name="modulate",
    )(x, sc, sh)


def _layer_norm(y, g, b):
    mu = jnp.mean(y, axis=-1, keepdims=True)
    dlt = y - mu
    var = jnp.mean(dlt * dlt, axis=-1, keepdims=True)
    return dlt * lax.rsqrt(var + NORM_EPS) * g + b


def _ln_mod_kernel(y_ref, g_ref, b_ref, sc_ref, sh_ref, x_ref, h_ref):
    xn = _layer_norm(y_ref[...], g_ref[...], b_ref[...])
    x_ref[...] = xn
    h_ref[...] = (xn * (1.0 + sc_ref[0]) + sh_ref[0]).astype(BF16)


def _ln_kernel(y_ref, g_ref, b_ref, x_ref):
    x_ref[...] = _layer_norm(y_ref[...], g_ref[...], b_ref[...])


def _ln(y, g, b, mod=None, rows_per_group=None):
    m, d = y.shape
    bs = _pick(rows_per_group or m, 256, 8)
    row = pl.BlockSpec((bs, d), lambda i: (i, 0))
    vec = pl.BlockSpec((1, d), lambda i: (0, 0))
    g2, b2 = g.reshape(1, d), b.reshape(1, d)
    if mod is None:
        return pl.pallas_call(
            _ln_kernel, grid=(m // bs,), in_specs=[row, vec, vec], out_specs=row,
            out_shape=jax.ShapeDtypeStruct((m, d), F32),
            compiler_params=_params(("arbitrary",)), name="layernorm",
        )(y, g2, b2)
    grp = pl.BlockSpec((1, 1, d), lambda i: (i * bs // rows_per_group, 0, 0))
    return pl.pallas_call(
        _ln_mod_kernel, grid=(m // bs,), in_specs=[row, vec, vec, grp, grp],
        out_specs=[row, row],
        out_shape=[jax.ShapeDtypeStruct((m, d), F32), jax.ShapeDtypeStruct((m, d), BF16)],
        compiler_params=_params(("arbitrary",)), name="layernorm_mod",
    )(y, g2, b2, mod[0], mod[1])


def _mm_kernel(*refs, nk, n_extra, epilogue, casts, grid):
    x_ref, w_ref = refs[0], refs[1]
    extra = refs[2:2 + n_extra]
    nc = len(casts)
    o_ref = refs[2 + n_extra + nc]
    k = pl.program_id(2)
    step = (pl.program_id(0) * grid[1] + pl.program_id(1)) * nk + k
    _run_casts(step, casts, refs[2 + n_extra:2 + n_extra + nc] + refs[3 + n_extra + nc:3 + n_extra + 2 * nc])
    part = _dot(x_ref[...], w_ref[0])
    if nk == 1:
        o_ref[...] = epilogue(part, *extra).astype(o_ref.dtype)
        return
    acc_ref = refs[3 + n_extra + 2 * nc]

    @pl.when(k == 0)
    def _():
        acc_ref[...] = part

    @pl.when((k > 0) & (k < nk - 1))
    def _():
        acc_ref[...] += part

    @pl.when(k == nk - 1)
    def _():
        o_ref[...] = epilogue(acc_ref[...] + part, *extra).astype(o_ref.dtype)


def _matmul_grid(m, n, kdim, bm, bn, bk):
    return (m // bm, n // bn, kdim // bk)


def _matmul(x, w, *, bm, bn, bk, out_dtype, epilogue, extra=(), extra_specs=(), casts=(), name):
    m, kdim = x.shape
    n = w.shape[2]
    grid = _matmul_grid(m, n, kdim, bm, bn, bk)
    nk = grid[2]
    kernel = functools.partial(_mm_kernel, nk=nk, n_extra=len(extra), epilogue=epilogue, casts=tuple(casts),
                               grid=grid)
    cast_specs = [cast.specs(lambda i, j, k: (i * grid[1] + j) * nk + k) for cast in casts]
    x_mode = pl.Buffered(1) if casts and nk == 1 else None
    res = pl.pallas_call(
        kernel,
        grid=grid,
        in_specs=[pl.BlockSpec((bm, bk), lambda i, j, k: (i, k), pipeline_mode=x_mode),
                  pl.BlockSpec((1, bk, bn), lambda i, j, k: (0, k, j)),
                  *extra_specs, *[sp[0] for sp in cast_specs]],
        out_specs=[pl.BlockSpec((bm, bn), lambda i, j, k: (i, j)), *[sp[1] for sp in cast_specs]],
        out_shape=[jax.ShapeDtypeStruct((m, n), out_dtype), *[cast.out_shape for cast in casts]],
        scratch_shapes=[pltpu.VMEM((bm, bn), F32)] if nk > 1 else [],
        compiler_params=_params(("arbitrary", "arbitrary", "arbitrary")),
        name=name,
    )(x, w, *extra, *[cast.w for cast in casts])
    return res if casts else res[0]


def _identity(acc):
    return acc


def _residual_epilogue(acc, x_ref, gate_ref, *, alpha):
    return alpha * x_ref[...] + gate_ref[0] * acc


def _matmul_residual(a, w, xres, gate, rows_per_group, alpha, *, bk, name):
    n = w.shape[2]
    bm = _pick(rows_per_group, 1024, 8)
    bn = _pick(n, 1024, LANES)
    return _matmul(
        a, w, bm=bm, bn=bn, bk=bk, out_dtype=F32,
        epilogue=functools.partial(_residual_epilogue, alpha=alpha),
        extra=(xres, gate),
        extra_specs=(pl.BlockSpec((bm, bn), lambda i, j, k: (i, j)),
                     pl.BlockSpec((1, 1, bn), lambda i, j, k: (i * bm // rows_per_group, 0, j))),
        name=name)


def _rope(x, cos, sin_signed):
    return x * cos + pltpu.roll(x, D_HEAD // 2, 1) * sin_signed


def _inproj_kernel(h_ref, w_ref, cos_ref, sin_ref, g_ref, *refs, casts, n_col_tiles):
    o_ref = refs[len(casts)]
    i, j = pl.program_id(0), pl.program_id(1)
    _run_casts(i * n_col_tiles + j, casts, refs[:len(casts)] + refs[len(casts) + 1:])
    acc = _dot(h_ref[...], w_ref[0])
    n_chunks = IN_TILE // D_HEAD
    t_ak, t_av, t_cq, t_ck, t_cv = (OFF_AK // IN_TILE, OFF_AV // IN_TILE, OFF_CQ // IN_TILE,
                                    OFF_CK // IN_TILE, OFF_CV // IN_TILE)

    @pl.when(j < t_av)
    def _():
        s = jnp.where(j < t_ak, Q_SCALE, 1.0).astype(F32)
        cos = cos_ref[...] * s
        sin = sin_ref[...] * s
        for c in range(n_chunks):
            sl = slice(c * D_HEAD, (c + 1) * D_HEAD)
            o_ref[:, sl] = _rope(acc[:, sl], cos, sin).astype(BF16)

    @pl.when(((j >= t_av) & (j < t_cq)) | (j >= t_cv))
    def _():
        o_ref[...] = acc.astype(BF16)

    @pl.when((j >= t_cq) & (j < t_cv))
    def _():
        is_q = j < t_ck
        g = jnp.where(is_q, g_ref[0:1, :], g_ref[1:2, :])
        s = jnp.where(is_q, Q_SCALE, 1.0).astype(F32)
        cos = cos_ref[...] * s
        sin = sin_ref[...] * s
        for c in range(n_chunks):
            sl = slice(c * D_HEAD, (c + 1) * D_HEAD)
            x = acc[:, sl]
            ms = jnp.mean(x * x, axis=-1, keepdims=True)
            xn = x * lax.rsqrt(ms + NORM_EPS) * g
            o_ref[:, sl] = _rope(xn, cos, sin).astype(BF16)


def _inproj_grid(m, rows_per_seq):
    bm = _pick(rows_per_seq, 1024, 8)
    return bm, (m // bm, D_IN // IN_TILE)


def _inproj(h, w, cos, sin_signed, qk_g, rows_per_seq, casts=()):
    m, d = h.shape
    bm, grid = _inproj_grid(m, rows_per_seq)
    nseq_blk = rows_per_seq // bm
    tab = pl.BlockSpec((bm, D_HEAD), lambda i, j: (i % nseq_blk, 0))
    cast_specs = [cast.specs(lambda i, j: i * grid[1] + j) for cast in casts]
    return pl.pallas_call(
        functools.partial(_inproj_kernel, casts=tuple(casts), n_col_tiles=grid[1]),
        grid=grid,
        in_specs=[pl.BlockSpec((bm, d), lambda i, j: (i, 0)),
                  pl.BlockSpec((1, d, IN_TILE), lambda i, j: (0, 0, j)),
                  tab, tab,
                  pl.BlockSpec((2, D_HEAD), lambda i, j: (0, 0)),
                  *[sp[0] for sp in cast_specs]],
        out_specs=[pl.BlockSpec((bm, IN_TILE), lambda i, j: (i, j)), *[sp[1] for sp in cast_specs]],
        out_shape=[jax.ShapeDtypeStruct((m, D_IN), BF16), *[cast.out_shape for cast in casts]],
        compiler_params=_params(("arbitrary", "arbitrary")),
        name="inproj",
    )(h, w, cos, sin_signed, qk_g, *[cast.w for cast in casts])


KEY_CHUNK = 1024


class _SoftmaxChain:
    def __init__(self, q, k_refs, k_lo, values, ones_at=None):
        self.q, self.k_refs, self.k_lo, self.values, self.ones_at = q, k_refs, k_lo, values, ones_at
        self.chunks = [(n, c0, min(c0 + KEY_CHUNK, k_ref.shape[0]))
                       for n, k_ref in enumerate(k_refs) for c0 in range(0, k_ref.shape[0], KEY_CHUNK)]
        self.scores = []
        self.lane_max = self.lane_sum = self.acc = self.row_max = None

    def score(self, i):
        n, c0, c1 = self.chunks[i]
        s = _dot_nt(self.q, self.k_refs[n][c0:c1, self.k_lo:self.k_lo + D_HEAD])
        self.scores.append(s)
        for t in range(0, c1 - c0, LANES):
            tile = s[:, t:t + LANES]
            self.lane_max = tile if self.lane_max is None else jnp.maximum(self.lane_max, tile)

    def accumulate(self, i):
        if self.row_max is None:
            self.row_max = jnp.max(self.lane_max, axis=-1, keepdims=True)
        n, c0, c1 = self.chunks[i]
        x = self.scores[i] - self.row_max
        if self.ones_at is None:
            e = jnp.exp2(x)
            for t in range(0, c1 - c0, LANES):
                tile = e[:, t:t + LANES]
                self.lane_sum = tile if self.lane_sum is None else self.lane_sum + tile
            e = e.astype(BF16)
        else:
            e = jnp.exp2(x.astype(BF16))
        pv = _dot(e, self.values[n][c0:c1])
        self.acc = pv if self.acc is None else self.acc + pv

    def result(self):
        if self.ones_at is None:
            return self.acc, jnp.sum(self.lane_sum, axis=-1, keepdims=True)
        return self.acc[:, :self.ones_at], self.acc[:, self.ones_at:self.ones_at + 1]


def _run_chains(n_blocks, make_chains, write):
    cur = make_chains(0)
    for i in range(len(cur[0].chunks)):
        for ch in cur:
            ch.score(i)
    for b in range(n_blocks):
        nxt = make_chains(b + 1) if b + 1 < n_blocks else []
        for i in range(len(cur[0].chunks)):
            for ch in nxt:
                ch.score(i)
            for ch in cur:
                ch.accumulate(i)
        write(b, cur)
        cur = nxt


def _diff_attn_kernel(*refs, n_src, lam_init):
    q_ref = refs[0]
    k_refs = refs[1:1 + n_src]
    v_refs = refs[1 + n_src:1 + 2 * n_src]
    lam_ref, g_ref, o_ref = refs[1 + 2 * n_src:]
    lp = lam_ref[...]
    lam = (jnp.exp(jnp.sum(lp[0:1] * lp[1:2], axis=-1, keepdims=True))
           - jnp.exp(jnp.sum(lp[2:3] * lp[3:4], axis=-1, keepdims=True)) + lam_init)
    rows = min(ATTN_ROWS, q_ref.shape[0])

    def make_chains(b):
        q = q_ref[b * rows:(b + 1) * rows, :]
        return [_SoftmaxChain(q[:, lo:lo + D_HEAD], k_refs, lo, v_refs) for lo in (0, D_HEAD)]

    def write(b, chains):
        (o1, den1), (o2, den2) = chains[0].result(), chains[1].result()
        o = o1 * (1.0 / den1) - o2 * (lam / den2)
        ms = jnp.mean(o * o, axis=-1, keepdims=True)
        y = (o * lax.rsqrt(ms + NORM_EPS)) * g_ref[...] * (1.0 - lam_init)
        o_ref[b * rows:(b + 1) * rows, :] = y.astype(BF16)

    _run_chains(q_ref.shape[0] // rows, make_chains, write)


def _diff_attention(pq, kv_sources, lam_p, subln_g, lam_init, batch, q_len, bq):
    nq = q_len // bq
    kblk, vblk = OFF_AK // DIFF_W, OFF_AV // DIFF_W
    k_specs = [pl.BlockSpec((n, DIFF_W), lambda b, h, i: (b, kblk + h)) for _, n in kv_sources]
    v_specs = [pl.BlockSpec((n, DIFF_W), lambda b, h, i: (b, vblk + h)) for _, n in kv_sources]
    srcs = [p for p, _ in kv_sources]
    kernel = functools.partial(_diff_attn_kernel, n_src=len(srcs), lam_init=lam_init)
    return pl.pallas_call(
        kernel,
        grid=(batch, H_DIFF, nq),
        in_specs=[pl.BlockSpec((bq, DIFF_W), lambda b, h, i: (b * nq + i, h)),
                  *k_specs, *v_specs,
                  pl.BlockSpec((4, D_HEAD), lambda b, h, i: (0, 0)),
                  pl.BlockSpec((1, DIFF_W), lambda b, h, i: (0, 0))],
        out_specs=pl.BlockSpec((bq, DIFF_W), lambda b, h, i: (b * nq + i, h)),
        out_shape=jax.ShapeDtypeStruct((batch * q_len, H_DIFF * DIFF_W), BF16),
        compiler_params=_params(("arbitrary", "arbitrary", "arbitrary")),
        name="diff_attention",
    )(pq, *srcs, *srcs, lam_p, subln_g.reshape(1, DIFF_W))


def _gqa_kernel(*refs, n_src):
    q_ref = refs[0]
    k_refs = refs[1:1 + n_src]
    v_refs = refs[1 + n_src:1 + 2 * n_src]
    o_ref = refs[1 + 2 * n_src]
    rows = min(ATTN_ROWS // 2, q_ref.shape[0])
    heads = [slice(g * D_HEAD, (g + 1) * D_HEAD) for g in range(GQA_GROUP)]
    values = [jnp.concatenate([v_ref[...], jnp.ones(v_ref.shape, BF16)], axis=1) for v_ref in v_refs]

    def make_chains(b):
        rs = slice(b * rows, (b + 1) * rows)
        q = jnp.concatenate([q_ref[rs, cs] for cs in heads], axis=0)
        return [_SoftmaxChain(q, k_refs, 0, values, ones_at=D_HEAD)]

    def write(b, chains):
        o, den = chains[0].result()
        o = (o / den).astype(BF16)
        for g, cs in enumerate(heads):
            o_ref[b * rows:(b + 1) * rows, cs] = o[g * rows:(g + 1) * rows]

    _run_chains(q_ref.shape[0] // rows, make_chains, write)


def _gqa_attention(pq, kv_sources, batch, q_len, bq):
    nq = q_len // bq
    qblk, kblk, vblk = OFF_CQ // GQA_W, OFF_CK // D_HEAD, OFF_CV // D_HEAD
    k_specs = [pl.BlockSpec((n, D_HEAD), lambda b, kh, i: (b, kblk + kh)) for _, n in kv_sources]
    v_specs = [pl.BlockSpec((n, D_HEAD), lambda b, kh, i: (b, vblk + kh)) for _, n in kv_sources]
    srcs = [p for p, _ in kv_sources]
    return pl.pallas_call(
        functools.partial(_gqa_kernel, n_src=len(srcs)),
        grid=(batch, H_KV, nq),
        in_specs=[pl.BlockSpec((bq, GQA_W), lambda b, kh, i: (b * nq + i, qblk + kh)),
                  *k_specs, *v_specs],
        out_specs=pl.BlockSpec((bq, GQA_W), lambda b, kh, i: (b * nq + i, kh)),
        out_shape=jax.ShapeDtypeStruct((batch * q_len, H_GQA * D_HEAD), BF16),
        compiler_params=_params(("arbitrary",) * 3),
        name="gqa_attention",
    )(pq, *srcs, *srcs)


def _dft_tables(n, split=64):
    k = jnp.arange(n, dtype=jnp.int32)

    def cos_sin(t):
        ang = ((k[:, None] * t[None, :]) % n).astype(F32) * (2.0 * math.pi / n)
        return jnp.cos(ang), jnp.sin(ang)

    if n <= split or n % split:
        c, s = cos_sin(k)
    else:
        ca, sa = (v[:, :, None] for v in cos_sin(jnp.arange(0, n, split, dtype=jnp.int32)))
        cb, sb = (v[:, None, :] for v in cos_sin(jnp.arange(split, dtype=jnp.int32)))
        c = (ca * cb - sa * sb).reshape(n, n)
        s = (sa * cb + ca * sb).reshape(n, n)
    return c.astype(BF16), s.astype(BF16)


FNET_RADIX = 4


def _fnet_twiddles(n):
    p = jnp.arange(1, FNET_RADIX, dtype=jnp.int32)[:, None]
    k = jnp.arange(n // FNET_RADIX, dtype=jnp.int32)[None, :]
    ang = ((p * k) % n).astype(F32) * (2.0 * math.pi / n)
    return jnp.cos(ang)[:, :, None], jnp.sin(ang)[:, :, None]


def _fnet_kernel(c_ref, s_ref, x0_ref, x1_ref, x2_ref, x3_ref, twc_ref, tws_ref, cc_ref, sc_ref, o_ref, *, scale):
    c, s = c_ref[...], s_ref[...]
    ys = []
    for p, x_ref in enumerate((x0_ref, x1_ref, x2_ref, x3_ref)):
        x = x_ref[...]
        zr, zi = _dot(c, x), -_dot(s, x)
        if p:
            tc, ts = twc_ref[p - 1], tws_ref[p - 1]
            zr, zi = zr * tc + zi * ts, zi * tc - zr * ts
        ys.append((zr, zi))
    (ar, ai), (br, bi), (cr, ci), (dr, di) = ys
    zs = ((ar + br + cr + dr, ai + bi + ci + di),
          (ar + bi - cr - di, ai - br - ci + dr),
          (ar - br + cr - dr, ai - bi + ci - di),
          (ar - bi - cr + di, ai + br - ci - dr))
    for q, (zr, zi) in enumerate(zs):
        zr, zi = zr.astype(BF16), zi.astype(BF16)
        for g in range(zr.shape[1] // FNET_GROUP_W):
            sl = slice(g * FNET_GROUP_W, (g + 1) * FNET_GROUP_W)
            y = _dot(zr[:, sl], cc_ref[...]) + _dot(zi[:, sl], sc_ref[...])
            o_ref[0, q, :, sl] = (y * scale).astype(BF16)


def _fourier_mix(p, batch, n, pos_tabs, twiddles, chan_tabs):
    nr = n // FNET_RADIX
    bk = _pick(nr, 512, 16)
    bc = 2 * FNET_GROUP_W
    width = FNET_GROUPS * FNET_GROUP_W
    pos = pl.BlockSpec((bk, nr), lambda b, i, j: (i, 0))
    tw = pl.BlockSpec((FNET_RADIX - 1, bk, 1), lambda b, i, j: (0, i, 0))
    chan = pl.BlockSpec((FNET_GROUP_W, FNET_GROUP_W), lambda b, i, j: (0, 0))
    p4 = p[:, OFF_B:OFF_B + width].reshape(batch * nr, FNET_RADIX * width)
    xs = [pl.BlockSpec((nr, bc), lambda b, i, j, r=r: (b, r * width // bc + j)) for r in range(FNET_RADIX)]
    scale = 1.0 / math.sqrt(n * FNET_GROUP_W)
    out = pl.pallas_call(
        functools.partial(_fnet_kernel, scale=scale),
        grid=(batch, nr // bk, width // bc),
        in_specs=[pos, pos, *xs, tw, tw, chan, chan],
        out_specs=pl.BlockSpec((1, FNET_RADIX, bk, bc), lambda b, i, j: (b, 0, i, j)),
        out_shape=jax.ShapeDtypeStruct((batch, FNET_RADIX, nr, width), BF16),
        compiler_params=_params(("arbitrary",) * 3),
        name="fourier_mix",
    )(*pos_tabs, p4, p4, p4, p4, *twiddles, *chan_tabs)
    return out.reshape(batch * n, width)


MERGE_TILE = 256


def _merge_kernel(h_ref, oa_ref, ob_ref, oc_ref, wg0_ref, wg1_ref, wg2_ref, bg0_ref, bg1_ref, bg2_ref,
                  wb0_ref, wb1_ref, wb2_ref, o_ref):
    h = h_ref[...]
    acc = None
    for br_ref, wg_ref, bg_ref, wb_ref in ((oa_ref, wg0_ref, bg0_ref, wb0_ref), (ob_ref, wg1_ref, bg1_ref, wb1_ref),
                                           (oc_ref, wg2_ref, bg2_ref, wb2_ref)):
        gate = jax.nn.sigmoid(_dot(h, wg_ref[0]) + bg_ref[0])
        term = gate * _dot(br_ref[...], wb_ref[0])
        acc = term if acc is None else acc + term
    o_ref[...] = acc.astype(BF16)


def _gated_merge(h, oa, ob, oc, w_gate, b_gate, w_branch, layer):
    m, d = h.shape
    bm = _pick(m, 1024, 8)
    bn = MERGE_TILE
    nj = d // bn
    once = pl.Buffered(1)
    br = pl.BlockSpec((bm, BRANCH_W), lambda i, j: (i, 0), pipeline_mode=once)
    wg = [pl.BlockSpec((1, d, bn), lambda i, j, n=n: (n * nj + j, 0, 0)) for n in range(N_BRANCH)]
    wb = [pl.BlockSpec((1, BRANCH_W, bn), lambda i, j, n=n: (j, n, 0)) for n in range(N_BRANCH)]
    bg = [pl.BlockSpec((1, 1, bn), lambda i, j, n=n: (layer, 0, n * nj + j)) for n in range(N_BRANCH)]
    return pl.pallas_call(
        _merge_kernel,
        grid=(m // bm, nj),
        in_specs=[pl.BlockSpec((bm, d), lambda i, j: (i, 0), pipeline_mode=once), br, br, br, *wg, *bg, *wb],
        out_specs=pl.BlockSpec((bm, bn), lambda i, j: (i, j)),
        out_shape=jax.ShapeDtypeStruct((m, d), BF16),
        compiler_params=_params(("arbitrary",) * 2),
        name="gated_merge",
    )(h, oa, ob, oc, w_gate, w_gate, w_gate, b_gate, b_gate, b_gate, w_branch, w_branch, w_branch)


CONV_EDGE_ROWS = 16


def _conv_act_kernel(a_ref, g_ref, w_ref, b_ref, o_ref):
    w, bias = w_ref[...], b_ref[...]
    s = a_ref.shape[1]
    edge = 2 * CONV_EDGE_ROWS

    def act(rows, first, last):
        a = a_ref[0, rows, :].astype(F32)
        n = a.shape[0]
        prev, nxt = pltpu.roll(a, 1, 0), pltpu.roll(a, n - 1, 0)
        row = lax.broadcasted_iota(jnp.int32, a.shape, 0)
        if first:
            prev = jnp.where(row == 0, 0.0, prev)
        if last:
            nxt = jnp.where(row == n - 1, 0.0, nxt)
        z = prev * w[0:1] + a * w[1:2] + nxt * w[2:3] + bias
        gelu = 0.5 * z * (1.0 + lax.erf(z * math.sqrt(0.5)))
        return (gelu * g_ref[0, rows, :].astype(F32)).astype(BF16)

    o_ref[0] = act(slice(0, s), False, False)
    o_ref[0, :CONV_EDGE_ROWS, :] = act(slice(0, edge), True, False)[:CONV_EDGE_ROWS]
    o_ref[0, s - CONV_EDGE_ROWS:, :] = act(slice(s - edge, s), False, True)[CONV_EDGE_ROWS:]


def _conv_act(u, conv_w, conv_b, batch, s):
    ffp = conv_w.shape[1]
    bc = _pick(ffp, 256, LANES)
    nj = ffp // bc
    u3 = u.reshape(batch, s, 2 * ffp)
    out = pl.pallas_call(
        _conv_act_kernel,
        grid=(batch, nj),
        in_specs=[pl.BlockSpec((1, s, bc), lambda b, j: (b, 0, j)),
                  pl.BlockSpec((1, s, bc), lambda b, j: (b, 0, nj + j)),
                  pl.BlockSpec((3, bc), lambda b, j: (0, j)),
                  pl.BlockSpec((1, bc), lambda b, j: (0, j))],
        out_specs=pl.BlockSpec((1, s, bc), lambda b, j: (b, 0, j)),
        out_shape=jax.ShapeDtypeStruct((batch, s, ffp), BF16),
        compiler_params=_params(("arbitrary", "arbitrary")),
        name="conv_act",
    )(u3, u3, conv_w, conv_b)
    return out.reshape(batch * s, ffp)


def _rope_tables(n):
    n_rows = n // GRID_W
    rows = jnp.broadcast_to(jnp.arange(n_rows)[:, None], (n_rows, GRID_W)).reshape(-1).astype(F32)
    cols = jnp.broadcast_to(jnp.arange(GRID_W)[None, :], (n_rows, GRID_W)).reshape(-1).astype(F32)
    quarter = D_HEAD // 4
    inv_freq = ROPE_THETA ** (-jnp.arange(quarter, dtype=F32) / quarter)
    ang = jnp.concatenate([rows[:, None] * inv_freq, cols[:, None] * inv_freq], axis=-1)
    ang = jnp.concatenate([ang, ang], axis=-1)
    sign = jnp.where(jnp.arange(D_HEAD) < D_HEAD // 2, -1.0, 1.0).astype(F32)
    return jnp.cos(ang), jnp.sin(ang) * sign


def kernel(x, c, ctx, c_ctx, w_ada, b_ada, w_in, diff_lambda, diff_subln_g, qk_norm_g, w_branch, w_gate,
           b_gate, w_o, ln1_g, ln1_b, w_up, conv_w, conv_b, w_down, ln2_g, ln2_b):
    batch, seq, d = x.shape
    n_ctx = ctx.shape[1]
    depth = w_ada.shape[0]
    ff = conv_w.shape[2]
    assert ff % LANES == 0 and w_in.shape[2] == D_IN
    ffp = -(-ff // 1024) * 1024 if ff > 1024 else ff
    alpha = (2 * depth) ** 0.25

    cvec = jnp.zeros((8, d), F32).at[:batch].set(c).at[batch].set(c_ctx)
    mod = _ada(cvec, w_ada, b_ada).reshape(depth, 8, 6, d)

    pad = ffp - ff
    conv_w_p = jnp.pad(conv_w, ((0, 0), (0, 0), (0, pad)))
    conv_b_p = jnp.pad(conv_b, ((0, 0), (0, pad))).reshape(depth, 1, ffp)
    b_gate3 = b_gate.reshape(depth, 1, N_BRANCH * d)
    w_branch2 = w_branch.reshape(depth, N_BRANCH * BRANCH_W, d)

    _, inproj_grid = _inproj_grid(batch * seq, seq)
    n_steps = inproj_grid[0] * inproj_grid[1]

    def early_casts(l, steps):
        return [_Cast(w_gate, l, steps, tile=MERGE_TILE), _Cast(w_branch2, l, steps, tile=MERGE_TILE),
                _Cast(w_o, l, steps), _Cast(w_down, l, steps, rp=ffp)]

    def up_cast(l):
        return _Cast(w_up, l, n_steps, nseg=2, cp=ffp)

    w_in_b = _cast_now(_Cast(w_in, 0, 32))
    layer_w = None

    cos_l, sin_l = _rope_tables(seq)
    cos_c, sin_c = jnp.ones((batch * n_ctx, D_HEAD), F32), jnp.zeros((batch * n_ctx, D_HEAD), F32)
    pos_l, pos_c = _dft_tables(seq // FNET_RADIX), _dft_tables(n_ctx // FNET_RADIX)
    tw_l, tw_c = _fnet_twiddles(seq), _fnet_twiddles(n_ctx)
    chan = _dft_tables(FNET_GROUP_W)

    xl = x.reshape(batch * seq, d)
    xc = ctx.reshape(batch * n_ctx, d)
    m_ctx = batch * n_ctx
    bq_diff_l = _pick(seq, 8 * ATTN_ROWS, 8)
    bq_gqa_l = _pick(seq, 4 * ATTN_ROWS, 8)
    bq_c = _pick(n_ctx, ATTN_ROWS, 8)
    bk_ff = _pick(ffp, 2816, LANES)

    def lat(i, l):
        return mod[l, :batch, i][:, None, :]

    def cx(i, l):
        return mod[l, batch, i][None, None, :]

    hl = _modulate(xl, lat(1, 0), lat(0, 0), seq)
    hc = _modulate(xc, cx(1, 0), cx(0, 0), m_ctx)

    for l in range(depth):
        last = l == depth - 1
        lam_init = 0.8 - 0.6 * math.exp(-0.3 * l)

        if layer_w is None:
            pl_, w_up_b, *layer_w = _inproj(hl, w_in_b, cos_l, sin_l, qk_norm_g[l], seq,
                                            casts=[up_cast(l), *early_casts(l, n_steps)])
        else:
            pl_, w_up_b = _inproj(hl, w_in_b, cos_l, sin_l, qk_norm_g[l], seq, casts=[up_cast(l)])
        w_gate_b, w_branch_b, w_o_b, w_down_b = layer_w
        (pc_,) = _inproj(hc, w_in_b, cos_c, sin_c, qk_norm_g[l], m_ctx)
        kv = [(pl_, seq), (pc_, n_ctx)]
        oa = _diff_attention(pl_, kv, diff_lambda[l], diff_subln_g[l], lam_init, batch, seq, bq_diff_l)
        ob = _fourier_mix(pl_, batch, seq, pos_l, tw_l, chan)
        oc = _gqa_attention(pl_, kv, batch, seq, bq_gqa_l)
        merged = _gated_merge(hl, oa, ob, oc, w_gate_b, b_gate3, w_branch_b, l)
        yl = _matmul_residual(merged, w_o_b, xl, lat(2, l), seq, alpha, bk=d, name="out_proj")
        xl, hl = _ln(yl, ln1_g[l], ln1_b[l], mod=(lat(4, l), lat(3, l)), rows_per_group=seq)

        if not last:
            kvc = [(pc_, n_ctx)]
            oa_c = _diff_attention(pc_, kvc, diff_lambda[l], diff_subln_g[l], lam_init, batch, n_ctx, bq_c)
            ob_c = _fourier_mix(pc_, batch, n_ctx, pos_c, tw_c, chan)
            oc_c = _gqa_attention(pc_, kvc, batch, n_ctx, bq_c)
            merged_c = _gated_merge(hc, oa_c, ob_c, oc_c, w_gate_b, b_gate3, w_branch_b, l)
            yc = _matmul_residual(merged_c, w_o_b, xc, cx(2, l), m_ctx, alpha, bk=d, name="out_proj_ctx")
            xc, hc = _ln(yc, ln1_g[l], ln1_b[l], mod=(cx(4, l), cx(3, l)), rows_per_group=m_ctx)

        def ffn(h, xres, gate, rows_per_group, s, convert_next=False):
            mrows = h.shape[0]
            bm, bn = _pick(rows_per_group, 1024, 8), _pick(2 * ffp, 1024, LANES)
            casts = ()
            if convert_next:
                g = _matmul_grid(mrows, 2 * ffp, d, bm, bn, d)
                steps = g[0] * g[1]
                casts = [_Cast(w_in, l + 1, steps), *early_casts(l + 1, steps)]
            res = _matmul(h, w_up_b, bm=bm, bn=bn, bk=d, out_dtype=BF16, epilogue=_identity, casts=casts,
                          name="ffn_up")
            u, *converted = res if casts else (res,)
            act = _conv_act(u, conv_w_p[l], conv_b_p[l], mrows // s, s)
            y = _matmul_residual(act, w_down_b, xres, gate, rows_per_group, alpha, bk=bk_ff, name="ffn_down")
            return y, converted

        yl, converted = ffn(hl, xl, lat(5, l), seq, seq, convert_next=not last)
        if last:
            xl = _ln(yl, ln2_g[l], ln2_b[l])
        else:
            w_in_b, *layer_w = converted
            xl, hl = _ln(yl, ln2_g[l], ln2_b[l], mod=(lat(1, l + 1), lat(0, l + 1)), rows_per_group=seq)
            yc, _ = ffn(hc, xc, cx(5, l), m_ctx, n_ctx)
            xc, hc = _ln(yc, ln2_g[l], ln2_b[l], mod=(cx(1, l + 1), cx(0, l + 1)), rows_per_group=m_ctx)

    return xl.reshape(batch, seq, d)
```
